```python
import jax, jax.numpy as jnp
from jax import lax
import numpy as np

D_MODEL = 2048
BATCH = 4
SEQ = 4096
DEPTH = 2

N_A_LAYERS = DEPTH // 2
N_B_LAYERS = DEPTH - N_A_LAYERS
MLA_HEADS = D_MODEL // 128
Q_LORA = D_MODEL // 4
KV_LORA = D_MODEL // 4
QK_NOPE = 128
QK_ROPE = 64
V_HEAD = 128
ROPE_THETA = 10000.0
FOX_HEADS = D_MODEL // 128
FOX_HEAD_DIM = 128
N_EXPERTS = 32
TOP_K = 4
D_FF = D_MODEL
SWIGLU_ALPHA = 1.702
SWIGLU_LIMIT = 7.0
Q_BLOCK = 128
RMS_EPS = 1e-6
N_MOD = 6

kernel_name = "yoco_mla_fox_moe_adaln"


def rms_norm(x, g):
    x32 = x.astype(jnp.float32)
    y = x32 * lax.rsqrt(jnp.mean(x32 * x32, axis=-1, keepdims=True) + RMS_EPS)
    return (y * g.astype(jnp.float32)).astype(x.dtype)


def apply_rope(x, cos, sin):
    xr = x.reshape(x.shape[:-1] + (x.shape[-1] // 2, 2))
    x1, x2 = xr[..., 0], xr[..., 1]
    out = jnp.stack([x1 * cos - x2 * sin, x1 * sin + x2 * cos], axis=-1)
    return out.reshape(x.shape)


def causal_attention(q, k, v, scale, log_decay=None):
    seq = q.shape[1]
    outs = []
    for i in range(seq // Q_BLOCK):
        q0 = i * Q_BLOCK
        kv_len = q0 + Q_BLOCK
        qb = q[:, q0:kv_len]
        kb = k[:, :kv_len]
        vb = v[:, :kv_len]
        s = jnp.einsum("bqhd,bkhd->bhqk", qb, kb, preferred_element_type=jnp.float32) * scale
        if log_decay is not None:
            s = s + (log_decay[:, :, q0:kv_len, None] - log_decay[:, :, None, :kv_len])
        causal = jnp.arange(kv_len)[None, :] <= (q0 + jnp.arange(Q_BLOCK))[:, None]
        s = jnp.where(causal, s, -jnp.inf)
        p = jax.nn.softmax(s, axis=-1).astype(v.dtype)
        outs.append(jnp.einsum("bhqk,bkhd->bqhd", p, vb))
    return jnp.concatenate(outs, axis=1)


def mla_mixer(h, cos, sin, w_in, q_norm_g, kv_norm_g, w_q_up, w_kv_up, w_out):
    b, s, _ = h.shape
    proj = h @ w_in
    c_q, c_kv, k_rope = jnp.split(proj, [Q_LORA, Q_LORA + KV_LORA], axis=-1)
    q = (rms_norm(c_q, q_norm_g) @ w_q_up).reshape(b, s, MLA_HEADS, QK_NOPE + QK_ROPE)
    q = jnp.concatenate([q[..., :QK_NOPE], apply_rope(q[..., QK_NOPE:], cos, sin)], axis=-1)
    kv = (rms_norm(c_kv, kv_norm_g) @ w_kv_up).reshape(b, s, MLA_HEADS, QK_NOPE + V_HEAD)
    k_nope, v = kv[..., :QK_NOPE], kv[..., QK_NOPE:]
    k_rope = apply_rope(k_rope[:, :, None, :], cos, sin)
    k = jnp.concatenate([k_nope, jnp.broadcast_to(k_rope, (b, s, MLA_HEADS, QK_ROPE))], axis=-1)
    o = causal_attention(q, k, v, (QK_NOPE + QK_ROPE) ** -0.5)
    return o.reshape(b, s, MLA_HEADS * V_HEAD) @ w_out


def shared_kv(x, norm_g, w_kvf, b_f):
    b, s, _ = x.shape
    hd = FOX_HEADS * FOX_HEAD_DIM
    proj = rms_norm(x, norm_g) @ w_kvf
    k, v, f_logit = jnp.split(proj, [hd, 2 * hd], axis=-1)
    k = k.reshape(b, s, FOX_HEADS, FOX_HEAD_DIM)
    v = v.reshape(b, s, FOX_HEADS, FOX_HEAD_DIM)
    log_f = jax.nn.log_sigmoid((f_logit + b_f).astype(jnp.float32))
    cum_log_f = jnp.cumsum(log_f, axis=1).transpose(0, 2, 1)
    return k, v, cum_log_f


def fox_mixer(h, k, v, cum_log_f, w_q, w_out):
    b, s, _ = h.shape
    q = (h @ w_q).reshape(b, s, FOX_HEADS, FOX_HEAD_DIM)
    o = causal_attention(q, k, v, FOX_HEAD_DIM ** -0.5, cum_log_f)
    return o.reshape(b, s, FOX_HEADS * FOX_HEAD_DIM) @ w_out


def clamped_swiglu(u):
    u = u.reshape(u.shape[:-1] + (D_FF, 2))
    glu = jnp.minimum(u[..., 0], SWIGLU_LIMIT)
    lin = jnp.clip(u[..., 1], -SWIGLU_LIMIT, SWIGLU_LIMIT)
    return glu * jax.nn.sigmoid(SWIGLU_ALPHA * glu) * (lin + 1.0)


def moe(h, router_w, router_b, w1, b1, w2, b2):
    b, s, d = h.shape
    t = h.reshape(b * s, d)
    logits = (t @ router_w + router_b).astype(jnp.float32)
    top_vals, top_idx = lax.top_k(logits, TOP_K)
    top_w = jax.nn.softmax(top_vals, axis=-1)
    combine = jnp.sum(jax.nn.one_hot(top_idx, N_EXPERTS, dtype=jnp.float32) * top_w[..., None], axis=1)
    combine = combine.astype(t.dtype)
    out = jnp.zeros_like(t)
    for e in range(N_EXPERTS):
        y = clamped_swiglu(t @ w1[e] + b1[e]) @ w2[e] + b2[e]
        out = out + combine[:, e:e + 1] * y
    return out.reshape(b, s, d)


def adaln(x, g, shift, scale):
    return rms_norm(x, g) * (1.0 + scale) + shift


def setup_inputs(seed: int = 0) -> dict:
    key = jax.random.key(seed)
    ks = jax.random.split(key, 32)
    f32 = jnp.float32
    hd = FOX_HEADS * FOX_HEAD_DIM

    def nrm(k, shape, scale):
        return jax.random.normal(k, shape, f32) * scale

    def gain(k, shape):
        return 1.0 + 0.02 * jax.random.normal(k, shape, f32)

    return {
        "x": nrm(ks[0], (BATCH, SEQ, D_MODEL), 1.0),
        "c": nrm(ks[1], (BATCH, D_MODEL), 1.0),
        "positions": jnp.broadcast_to(jnp.arange(SEQ, dtype=jnp.int32), (BATCH, SEQ)),
        "mod_w": nrm(ks[2], (DEPTH, D_MODEL, N_MOD * D_MODEL), 0.5 * D_MODEL ** -0.5),
        "mod_b": nrm(ks[3], (DEPTH, N_MOD * D_MODEL), 0.02),
        "attn_norm_g": gain(ks[4], (DEPTH, D_MODEL)),
        "ffn_norm_g": gain(ks[5], (DEPTH, D_MODEL)),
        "mla_w_in": nrm(ks[6], (N_A_LAYERS, D_MODEL, Q_LORA + KV_LORA + QK_ROPE), D_MODEL ** -0.5),
        "mla_q_norm_g": gain(ks[7], (N_A_LAYERS, Q_LORA)),
        "mla_kv_norm_g": gain(ks[8], (N_A_LAYERS, KV_LORA)),
        "mla_w_q_up": nrm(ks[9], (N_A_LAYERS, Q_LORA, MLA_HEADS * (QK_NOPE + QK_ROPE)), Q_LORA ** -0.5),
        "mla_w_kv_up": nrm(ks[10], (N_A_LAYERS, KV_LORA, MLA_HEADS * (QK_NOPE + V_HEAD)), KV_LORA ** -0.5),
        "mla_w_out": nrm(ks[11], (N_A_LAYERS, MLA_HEADS * V_HEAD, D_MODEL), (MLA_HEADS * V_HEAD) ** -0.5),
        "shared_norm_g": gain(ks[12], (D_MODEL,)),
        "shared_w_kvf": nrm(ks[13], (D_MODEL, 2 * hd + FOX_HEADS), D_MODEL ** -0.5),
        "shared_b_f": jax.random.uniform(ks[14], (FOX_HEADS,), f32, minval=1.0, maxval=6.0),
        "fox_w_q": nrm(ks[15], (N_B_LAYERS, D_MODEL, hd), D_MODEL ** -0.5),
        "fox_w_out": nrm(ks[16], (N_B_LAYERS, hd, D_MODEL), hd ** -0.5),
        "router_w": nrm(ks[17], (DEPTH, D_MODEL, N_EXPERTS), D_MODEL ** -0.5),
        "router_b": nrm(ks[18], (DEPTH, N_EXPERTS), 0.01),
        "moe_w1": nrm(ks[19], (DEPTH, N_EXPERTS, D_MODEL, 2 * D_FF), D_MODEL ** -0.5),
        "moe_b1": nrm(ks[20], (DEPTH, N_EXPERTS, 2 * D_FF), 0.02),
        "moe_w2": nrm(ks[21], (DEPTH, N_EXPERTS, D_FF, D_MODEL), D_FF ** -0.5),
        "moe_b2": nrm(ks[22], (DEPTH, N_EXPERTS, D_MODEL), 0.02),
        "final_norm_g": gain(ks[23], (D_MODEL,)),
    }


def reference(x, c, positions, mod_w, mod_b, attn_norm_g, ffn_norm_g,
              mla_w_in, mla_q_norm_g, mla_kv_norm_g, mla_w_q_up, mla_w_kv_up, mla_w_out,
              shared_norm_g, shared_w_kvf, shared_b_f, fox_w_q, fox_w_out,
              router_w, router_b, moe_w1, moe_b1, moe_w2, moe_b2, final_norm_g):
    inv_freq = ROPE_THETA ** (-jnp.arange(0, QK_ROPE, 2, dtype=jnp.float32) / QK_ROPE)
    ang = positions.astype(jnp.float32)[..., None] * inv_freq
    cos = jnp.cos(ang)[:, :, None, :].astype(x.dtype)
    sin = jnp.sin(ang)[:, :, None, :].astype(x.dtype)

    h = x
    k_sh = v_sh = cum_log_f = None
    for layer in range(DEPTH):
        mod = (c @ mod_w[layer] + mod_b[layer])[:, None, :]
        sh_a, sc_a, g_a, sh_f, sc_f, g_f = jnp.split(mod, N_MOD, axis=-1)
        a_in = adaln(h, attn_norm_g[layer], sh_a, sc_a)
        if layer < N_A_LAYERS:
            a_out = mla_mixer(a_in, cos, sin, mla_w_in[layer], mla_q_norm_g[layer],
                              mla_kv_norm_g[layer], mla_w_q_up[layer], mla_w_kv_up[layer],
                              mla_w_out[layer])
        else:
            j = layer - N_A_LAYERS
            a_out = fox_mixer(a_in, k_sh, v_sh, cum_log_f, fox_w_q[j], fox_w_out[j])
        h = h + g_a * a_out
        f_in = adaln(h, ffn_norm_g[layer], sh_f, sc_f)
        h = h + g_f * moe(f_in, router_w[layer], router_b[layer], moe_w1[layer],
                          moe_b1[layer], moe_w2[layer], moe_b2[layer])
        if layer == N_A_LAYERS - 1:
            k_sh, v_sh, cum_log_f = shared_kv(h, shared_norm_g, shared_w_kvf, shared_b_f)
    return rms_norm(h, final_norm_g)
```

```python
import functools

import jax
import jax.numpy as jnp
from jax import lax
from jax.experimental import pallas as pl
from jax.experimental.pallas import tpu as pltpu

F32 = jnp.float32
BF16 = jnp.bfloat16
HIGHEST = lax.Precision.HIGHEST

RMS_EPS = 1e-6
HEAD_DIM = 128
ROPE_DIM = 64
ROPE_THETA = 10000.0
TOP_K = 4
SWIGLU_ALPHA = 1.702
SWIGLU_LIMIT = 7.0
N_MOD = 6

V7X_LANES = 128
V7X_SUBLANES = 8
V7X_MXU_DIM = 256
V7X_VMEM_BYTES = 64 * 1024 * 1024
MIB = 1024 * 1024

NEG_BIG = -1e30


def _params(semantics, vmem_mib):
    assert vmem_mib * MIB < V7X_VMEM_BYTES
    return pltpu.CompilerParams(dimension_semantics=semantics,
                                vmem_limit_bytes=vmem_mib * MIB)


def _tile(n, pref):
    if n <= pref:
        return n
    t = pref
    while n % t:
        t //= 2
    return t


def _rms(x, g):
    ms = jnp.mean(x * x, axis=-1, keepdims=True)
    return x * lax.rsqrt(ms + RMS_EPS) * g


def _mod_kernel(c_ref, w_ref, b_ref, o_ref):
    o_ref[...] = jnp.dot(c_ref[...], w_ref[...], precision=HIGHEST,
                         preferred_element_type=F32) + b_ref[...]


def _modulation(c_pad, mod_w, mod_b):
    n_layers, d, n = mod_w.shape
    rows = c_pad.shape[0]
    tn = _tile(n, 1024)
    return pl.pallas_call(
        _mod_kernel,
        grid=(n_layers, n // tn),
        in_specs=[
            pl.BlockSpec((rows, d), lambda l, j: (0, 0)),
            pl.BlockSpec((None, d, tn), lambda l, j: (l, 0, j)),
            pl.BlockSpec((None, 1, tn), lambda l, j: (l, 0, j)),
        ],
        out_specs=pl.BlockSpec((None, rows, tn), lambda l, j: (l, 0, j)),
        out_shape=jax.ShapeDtypeStruct((n_layers, rows, n), F32),
        compiler_params=_params(("arbitrary", "arbitrary"), 40),
        name="modulation",
    )(c_pad, mod_w, mod_b.reshape(n_layers, 1, n))


def _norm_matmul_kernel(*refs, modulate, out_scale):
    if modulate:
        h_ref, g_ref, sc_ref, sh_ref, w_ref, o_ref = refs
    else:
        h_ref, g_ref, w_ref, o_ref = refs
    y = _rms(h_ref[...], g_ref[...])
    if modulate:
        y = y * (1.0 + sc_ref[...]) + sh_ref[...]
    w = w_ref[...]
    if w.dtype == BF16:
        out = jnp.dot(y.astype(BF16), w, preferred_element_type=F32)
    else:
        out = jnp.dot(y, w, precision=HIGHEST, preferred_element_type=F32)
    if out_scale != 1.0:
        out = out * out_scale
    o_ref[...] = out.astype(o_ref.dtype)


def _norm_matmul(h, g, w, seq, *, scale=None, shift=None, out_dtype=F32, out_scale=1.0):
    t, d = h.shape
    n = w.shape[1]
    tm = _tile(seq, 256)
    tn = _tile(n, 2048)
    per_batch = seq // tm
    modulate = scale is not None
    in_specs = [pl.BlockSpec((tm, d), lambda j, i: (i, 0)),
                pl.BlockSpec((1, d), lambda j, i: (0, 0))]
    args = [h, g.reshape(1, d)]
    if modulate:
        mod_spec = pl.BlockSpec((None, 1, d), lambda j, i: (i // per_batch, 0, 0))
        in_specs += [mod_spec, mod_spec]
        args += [scale, shift]
    in_specs.append(pl.BlockSpec((d, tn), lambda j, i: (0, j)))
    args.append(w)
    return pl.pallas_call(
        functools.partial(_norm_matmul_kernel, modulate=modulate, out_scale=out_scale),
        grid=(n // tn, t // tm),
        in_specs=in_specs,
        out_specs=pl.BlockSpec((tm, tn), lambda j, i: (i, j)),
        out_shape=jax.ShapeDtypeStruct((t, n), out_dtype),
        compiler_params=_params(("arbitrary", "arbitrary"), 48),
        name="norm_matmul",
    )(*args)


def _rope_table_kernel(pos_ref, freq_ref, c_ref, s1_ref, s2_ref):
    half = ROPE_DIM // 2
    ang = pos_ref[...].astype(F32) * freq_ref[...]
    lane = lax.broadcasted_iota(jnp.int32, ang.shape, 1)
    cos = jnp.cos(ang)
    sin = jnp.sin(ang)
    c_ref[...] = jnp.where(lane < 2 * half, cos, 0.0)
    s1_ref[...] = jnp.where(lane < half, -sin, 0.0)
    s2_ref[...] = jnp.where((lane >= half) & (lane < 2 * half), sin, 0.0)


def _rope_tables(positions):
    t = positions.size
    half = ROPE_DIM // 2
    inv_freq = ROPE_THETA ** (-jnp.arange(0, ROPE_DIM, 2, dtype=F32) / ROPE_DIM)
    freq = jnp.tile(inv_freq, V7X_LANES // half).reshape(1, V7X_LANES)
    tm = _tile(t, 512)
    spec = pl.BlockSpec((tm, V7X_LANES), lambda i: (i, 0))
    shape = jax.ShapeDtypeStruct((t, V7X_LANES), F32)
    return pl.pallas_call(
        _rope_table_kernel,
        grid=(t // tm,),
        in_specs=[pl.BlockSpec((tm, 1), lambda i: (i, 0)),
                  pl.BlockSpec((1, V7X_LANES), lambda i: (0, 0))],
        out_specs=[spec, spec, spec],
        out_shape=[shape, shape, shape],
        compiler_params=_params(("arbitrary",), 16),
        name="rope_tables",
    )(positions.reshape(t, 1), freq)


def _rope_chunk(z, c, s1, s2):
    half = ROPE_DIM // 2
    return (z * c + pltpu.roll(z, V7X_LANES - half, axis=1) * s1
            + pltpu.roll(z, half, axis=1) * s2)


def _mla_q_kernel(x_ref, g_ref, w_ref, c_ref, s1_ref, s2_ref, o_ref, *, heads, scale):
    y = _rms(x_ref[...], g_ref[...]).astype(BF16)
    q = jnp.dot(y, w_ref[...], preferred_element_type=F32) * scale
    c, s1, s2 = c_ref[...], s1_ref[...], s2_ref[...]
    for h in range(heads):
        base = h * 2 * HEAD_DIM
        o_ref[:, base:base + HEAD_DIM] = q[:, base:base + HEAD_DIM].astype(BF16)
        z = q[:, base + HEAD_DIM:base + 2 * HEAD_DIM]
        o_ref[:, base + HEAD_DIM:base + 2 * HEAD_DIM] = _rope_chunk(z, c, s1, s2).astype(BF16)


def _mla_kv_kernel(x_ref, r_ref, g_ref, w_ref, c_ref, s1_ref, s2_ref, k_ref, v_ref, *, heads):
    y = _rms(x_ref[...], g_ref[...]).astype(BF16)
    kv = jnp.dot(y, w_ref[...], preferred_element_type=F32)
    kr = _rope_chunk(r_ref[...], c_ref[...], s1_ref[...], s2_ref[...]).astype(BF16)
    for h in range(heads):
        base = h * 2 * HEAD_DIM
        k_ref[:, base:base + HEAD_DIM] = kv[:, h * HEAD_DIM:(h + 1) * HEAD_DIM].astype(BF16)
        k_ref[:, base + HEAD_DIM:base + 2 * HEAD_DIM] = kr
    v_ref[...] = kv[:, heads * HEAD_DIM:].astype(BF16)


def _mla_qkv(proj, q_norm_g, kv_norm_g, wq, wkv, tables, heads, q_lora, kv_lora, scale):
    t = proj.shape[0]
    tm = _tile(t, 256)
    dq = heads * 2 * HEAD_DIM
    assert q_lora == kv_lora and q_lora % V7X_LANES == 0
    tab_spec = pl.BlockSpec((tm, V7X_LANES), lambda i: (i, 0))
    rope_block = (q_lora + kv_lora) // V7X_LANES
    q = pl.pallas_call(
        functools.partial(_mla_q_kernel, heads=heads, scale=scale),
        grid=(t // tm,),
        in_specs=[pl.BlockSpec((tm, q_lora), lambda i: (i, 0)),
                  pl.BlockSpec((1, q_lora), lambda i: (0, 0)),
                  pl.BlockSpec((q_lora, dq), lambda i: (0, 0)),
                  tab_spec, tab_spec, tab_spec],
        out_specs=pl.BlockSpec((tm, dq), lambda i: (i, 0)),
        out_shape=jax.ShapeDtypeStruct((t, dq), BF16),
        compiler_params=_params(("arbitrary",), 40),
        name="mla_q",
    )(proj, q_norm_g.reshape(1, q_lora), wq, *tables)
    k, v = pl.pallas_call(
        functools.partial(_mla_kv_kernel, heads=heads),
        grid=(t // tm,),
        in_specs=[pl.BlockSpec((tm, kv_lora), lambda i: (i, 1)),
                  pl.BlockSpec((tm, V7X_LANES), lambda i: (i, rope_block)),
                  pl.BlockSpec((1, kv_lora), lambda i: (0, 0)),
                  pl.BlockSpec((kv_lora, dq), lambda i: (0, 0)),
                  tab_spec, tab_spec, tab_spec],
        out_specs=[pl.BlockSpec((tm, dq), lambda i: (i, 0)),
                   pl.BlockSpec((tm, heads * HEAD_DIM), lambda i: (i, 0))],
        out_shape=[jax.ShapeDtypeStruct((t, dq), BF16),
                   jax.ShapeDtypeStruct((t, heads * HEAD_DIM), BF16)],
        compiler_params=_params(("arbitrary",), 40),
        name="mla_kv",
    )(proj, proj, kv_norm_g.reshape(1, kv_lora), wkv, *tables)
    return q, k, v


def _flash_kernel(*refs, tq, decay):
    if decay:
        q_ref, k_ref, v_ref, fq_ref, fk_ref, o_ref = refs
    else:
        q_ref, k_ref, v_ref, o_ref = refs
    qi = pl.program_id(2)
    q = q_ref[...]
    if decay:
        head = pl.program_id(1)
        fq_all = fq_ref[...]
        lane = lax.broadcasted_iota(jnp.int32, fq_all.shape, 1)
        fq = jnp.sum(jnp.where(lane == head, fq_all, 0.0), axis=-1, keepdims=True)

    def block(j, carry, masked):
        m, l, acc = carry
        start = pl.multiple_of(j * tq, tq)
        k = k_ref[pl.ds(start, tq), :]
        v = v_ref[pl.ds(start, tq), :]
        s = lax.dot_general(q, k, (((1,), (1,)), ((), ())), preferred_element_type=F32)
        if decay:
            s = s + (fq - fk_ref[:, pl.ds(start, tq)])
        if masked:
            row = lax.broadcasted_iota(jnp.int32, s.shape, 0)
            col = lax.broadcasted_iota(jnp.int32, s.shape, 1)
            s = jnp.where(col <= row, s, -jnp.inf)
        m_new = jnp.maximum(m, jnp.max(s, axis=-1, keepdims=True))
        p = jnp.exp(s - m_new)
        alpha = jnp.exp(m - m_new)
        l = alpha * l + jnp.sum(p, axis=-1, keepdims=True)
        acc = alpha * acc + jnp.dot(p.astype(BF16), v, preferred_element_type=F32)
        return m_new, l, acc

    init = (jnp.full((tq, 1), -jnp.inf, F32), jnp.zeros((tq, 1), F32),
            jnp.zeros((tq, v_ref.shape[-1]), F32))
    carry = lax.fori_loop(0, qi, lambda j, c: block(j, c, False), init)
    _, l, acc = block(qi, carry, True)
    o_ref[...] = (acc / l).astype(o_ref.dtype)


def _flash_attention(q, k, v, heads, dv, v_block0=0, fq=None, fk=None):
    b, s, _ = q.shape
    dk = q.shape[-1] // heads
    tq = _tile(s, 512)
    decay = fq is not None
    in_specs = [pl.BlockSpec((None, tq, dk), lambda bi, h, i: (bi, i, h)),
                pl.BlockSpec((None, s, dk), lambda bi, h, i: (bi, 0, h)),
                pl.BlockSpec((None, s, dv), lambda bi, h, i: (bi, 0, v_block0 + h))]
    args = [q, k, v]
    if decay:
        in_specs += [pl.BlockSpec((None, tq, V7X_LANES), lambda bi, h, i: (bi, i, 0)),
                     pl.BlockSpec((None, None, 1, s), lambda bi, h, i: (bi, h, 0, 0))]
        args += [fq, fk]
    return pl.pallas_call(
        functools.partial(_flash_kernel, tq=tq, decay=decay),
        grid=(b, heads, s // tq),
        in_specs=in_specs,
        out_specs=pl.BlockSpec((None, tq, dv), lambda bi, h, i: (bi, i, h)),
        out_shape=jax.ShapeDtypeStruct((b, s, heads * dv), BF16),
        compiler_params=_params(("arbitrary", "arbitrary", "arbitrary"), 40),
        name="flash_attention",
    )(*args)


def _out_proj_kernel(o_ref, w_ref, h_ref, g_ref, out_ref):
    a = jnp.dot(o_ref[...], w_ref[...], preferred_element_type=F32)
    out_ref[...] = h_ref[...] + g_ref[...] * a


def _out_proj_residual(o, w, h, gate, seq):
    t, dk = o.shape
    d = w.shape[1]
    tm = _tile(seq, 256)
    per_batch = seq // tm
    return pl.pallas_call(
        _out_proj_kernel,
        grid=(t // tm,),
        in_specs=[pl.BlockSpec((tm, dk), lambda i: (i, 0)),
                  pl.BlockSpec((dk, d), lambda i: (0, 0)),
                  pl.BlockSpec((tm, d), lambda i: (i, 0)),
                  pl.BlockSpec((None, 1, d), lambda i: (i // per_batch, 0, 0))],
        out_specs=pl.BlockSpec((tm, d), lambda i: (i, 0)),
        out_shape=jax.ShapeDtypeStruct((t, d), F32),
        compiler_params=_params(("arbitrary",), 40),
        name="out_proj_residual",
    )(o, w, h, gate)


def _forget_cumsum_kernel(f_ref, b_ref, o_ref, carry_ref):
    @pl.when(pl.program_id(1) == 0)
    def _():
        carry_ref[...] = jnp.zeros_like(carry_ref)

    x = f_ref[...] + b_ref[...]
    log_f = jnp.minimum(x, 0.0) - jnp.log1p(jnp.exp(-jnp.abs(x)))
    ts = x.shape[0]
    row = lax.broadcasted_iota(jnp.int32, (ts, ts), 0)
    col = lax.broadcasted_iota(jnp.int32, (ts, ts), 1)
    tri = jnp.where(col <= row, 1.0, 0.0).astype(F32)
    c = jnp.dot(tri, log_f, precision=HIGHEST, preferred_element_type=F32) + carry_ref[...]
    o_ref[...] = c
    carry_ref[...] = c[ts - 1:ts, :]


def _forget_cumsum(f_logit, b_pad):
    b, s, lanes = f_logit.shape
    ts = _tile(s, 512)
    return pl.pallas_call(
        _forget_cumsum_kernel,
        grid=(b, s // ts),
        in_specs=[pl.BlockSpec((None, ts, lanes), lambda bi, i: (bi, i, 0)),
                  pl.BlockSpec((1, lanes), lambda bi, i: (0, 0))],
        out_specs=pl.BlockSpec((None, ts, lanes), lambda bi, i: (bi, i, 0)),
        out_shape=jax.ShapeDtypeStruct((b, s, lanes), F32),
        scratch_shapes=[pltpu.VMEM((1, lanes), F32)],
        compiler_params=_params(("arbitrary", "arbitrary"), 16),
        name="forget_cumsum",
    )(f_logit, b_pad)


def _moe_route_kernel(h_ref, g_ref, sc_ref, sh_ref, rw_ref, rb_ref, f_ref, idx_ref, w_ref):
    y = _rms(h_ref[...], g_ref[...]) * (1.0 + sc_ref[...]) + sh_ref[...]
    f_ref[...] = y
    logits = jnp.dot(y, rw_ref[...], precision=HIGHEST,
                     preferred_element_type=F32) + rb_ref[...]
    lane = lax.broadcasted_iota(jnp.int32, logits.shape, 1).astype(F32)
    vals, idxs = [], []
    for _ in range(TOP_K):
        m = jnp.max(logits, axis=-1, keepdims=True)
        ix = jnp.min(jnp.where(logits == m, lane, float(logits.shape[1])), axis=-1,
                     keepdims=True)
        vals.append(m)
        idxs.append(ix)
        logits = jnp.where(lane == ix, -jnp.inf, logits)
    exps = [jnp.exp(v - vals[0]) for v in vals]
    den = exps[0]
    for e in exps[1:]:
        den = den + e
    slot = lax.broadcasted_iota(jnp.int32, idx_ref.shape, 1)
    idx_out = jnp.zeros(idx_ref.shape, jnp.int32)
    w_out = jnp.zeros(w_ref.shape, F32)
    for k in range(TOP_K):
        idx_out = jnp.where(slot == k, idxs[k].astype(jnp.int32), idx_out)
        w_out = jnp.where(slot == k, exps[k] / den, w_out)
    idx_ref[...] = idx_out
    w_ref[...] = w_out


def _moe_route(h, g, scale, shift, rw_pad, rb_pad, seq):
    t, d = h.shape
    tm = _tile(seq, 256)
    per_batch = seq // tm
    lanes = rw_pad.shape[1]
    mod_spec = pl.BlockSpec((None, 1, d), lambda i: (i // per_batch, 0, 0))
    return pl.pallas_call(
        _moe_route_kernel,
        grid=(t // tm,),
        in_specs=[pl.BlockSpec((tm, d), lambda i: (i, 0)),
                  pl.BlockSpec((1, d), lambda i: (0, 0)),
                  mod_spec, mod_spec,
                  pl.BlockSpec((d, lanes), lambda i: (0, 0)),
                  pl.BlockSpec((1, lanes), lambda i: (0, 0))],
        out_specs=[pl.BlockSpec((tm, d), lambda i: (i, 0)),
                   pl.BlockSpec((tm, TOP_K), lambda i: (i, 0)),
                   pl.BlockSpec((tm, TOP_K), lambda i: (i, 0))],
        out_shape=[jax.ShapeDtypeStruct((t, d), F32),
                   jax.ShapeDtypeStruct((t, TOP_K), jnp.int32),
                   jax.ShapeDtypeStruct((t, TOP_K), F32)],
        compiler_params=_params(("arbitrary",), 32),
        name="moe_route",
    )(h, g.reshape(1, d), scale, shift, rw_pad, rb_pad)


def _routing_tables(top_idx, n_experts, tm):
    t, k = top_idx.shape
    n_tiles = (t * k) // tm + n_experts
    e_flat = top_idx.reshape(-1)
    onehot = (e_flat[:, None] == jnp.arange(n_experts, dtype=jnp.int32)[None, :]).astype(jnp.int32)
    incl = jnp.cumsum(onehot, axis=0)
    counts = incl[-1]
    rank = jnp.sum((incl - onehot) * onehot, axis=1)
    tiles_per = (counts + tm - 1) // tm
    tile_end = jnp.cumsum(tiles_per)
    row_start = (tile_end - tiles_per) * tm
    pos = row_start[e_flat] + rank
    n_used = tile_end[-1]
    tile_ids = jnp.minimum(jnp.arange(n_tiles, dtype=jnp.int32), n_used - 1)
    tile_expert = jnp.sum((tile_ids[:, None] >= tile_end[None, :]).astype(jnp.int32), axis=1)
    tok = jnp.arange(t * k, dtype=jnp.int32) // k
    src_tok = jnp.zeros((n_tiles * tm,), jnp.int32).at[pos].set(tok)
    return (src_tok, pos.reshape(t, k), tile_expert.astype(jnp.int32),
            n_used.reshape(1).astype(jnp.int32))


def _row_copy(src_hbm, row, buf, slot, r, sem):
    return pltpu.make_async_copy(src_hbm.at[pl.ds(row, 1), :],
                                 buf.at[slot, pl.ds(r, 1), :], sem.at[slot])


def _dispatch_kernel(nt_ref, cur_ref, nxt_ref, f_hbm, o_ref, buf, sem, *, tm):
    i = pl.program_id(0)
    nt = nt_ref[0]

    def issue(idx_ref, slot):
        def body(r, c):
            _row_copy(f_hbm, idx_ref[0, 0, r], buf, slot, r, sem).start()
            return c
        lax.fori_loop(0, tm, body, 0)

    @pl.when(i == 0)
    def _():
        issue(cur_ref, 0)

    @pl.when(i + 1 < nt)
    def _():
        issue(nxt_ref, (i + 1) % 2)

    @pl.when(i < nt)
    def _():
        slot = i % 2

        def wait_body(r, c):
            _row_copy(f_hbm, 0, buf, slot, r, sem).wait()
            return c
        lax.fori_loop(0, tm, wait_body, 0)
        o_ref[...] = buf[slot].astype(o_ref.dtype)

    @pl.when(i >= nt)
    def _():
        o_ref[...] = jnp.zeros_like(o_ref)


def _moe_dispatch(f_in, src_tok, n_used, tm):
    t, d = f_in.shape
    n_tiles = src_tok.shape[0] // tm
    src3 = src_tok.reshape(n_tiles, 1, tm)
    smem = functools.partial(pl.BlockSpec, memory_space=pltpu.SMEM)
    grid_spec = pltpu.PrefetchScalarGridSpec(
        num_scalar_prefetch=1,
        grid=(n_tiles,),
        in_specs=[smem((1, 1, tm), lambda i, nt: (i, 0, 0)),
                  smem((1, 1, tm), lambda i, nt: (jnp.minimum(i + 1, n_tiles - 1), 0, 0)),
                  pl.BlockSpec(memory_space=pl.ANY)],
        out_specs=pl.BlockSpec((tm, d), lambda i, nt: (i, 0)),
        scratch_shapes=[pltpu.VMEM((2, tm, d), F32), pltpu.SemaphoreType.DMA((2,))],
    )
    return pl.pallas_call(
        functools.partial(_dispatch_kernel, tm=tm),
        grid_spec=grid_spec,
        out_shape=jax.ShapeDtypeStruct((n_tiles * tm, d), BF16),
        compiler_params=_params(("arbitrary",), 32),
        name="moe_dispatch",
    )(n_used, src3, src3, f_in)


def _expert_changed(te_ref, i):
    return (i == 0) | (te_ref[i] != te_ref[jnp.maximum(i - 1, 0)])


def _moe_up_kernel(te_ref, nt_ref, x_ref, w_ref, bg_ref, bl_ref, o_ref, wbf_ref):
    i = pl.program_id(1)
    active = i < nt_ref[0]
    half = wbf_ref.shape[1] // 2
    group = V7X_MXU_DIM

    @pl.when(active & _expert_changed(te_ref, i))
    def _():
        src = lax.broadcasted_iota(jnp.int32, (group, group), 0)
        dst = lax.broadcasted_iota(jnp.int32, (group, group), 1)
        want = jnp.where(dst < group // 2, 2 * dst, 2 * (dst - group // 2) + 1)
        perm = jnp.where(src == want, 1.0, 0.0).astype(BF16)
        for gi in range(wbf_ref.shape[1] // group):
            blk = w_ref[:, gi * group:(gi + 1) * group].astype(BF16)
            split = jnp.dot(blk, perm, preferred_element_type=F32).astype(BF16)
            lo = gi * (group // 2)
            wbf_ref[:, lo:lo + group // 2] = split[:, :group // 2]
            wbf_ref[:, half + lo:half + lo + group // 2] = split[:, group // 2:]

    @pl.when(active)
    def _():
        u = jnp.dot(x_ref[...], wbf_ref[...], preferred_element_type=F32)
        glu = jnp.minimum(u[:, :half] + bg_ref[...], SWIGLU_LIMIT)
        lin = jnp.clip(u[:, half:] + bl_ref[...], -SWIGLU_LIMIT, SWIGLU_LIMIT)
        act = glu * jax.nn.sigmoid(SWIGLU_ALPHA * glu) * (lin + 1.0)
        o_ref[...] = act.astype(o_ref.dtype)

    @pl.when(jnp.logical_not(active))
    def _():
        o_ref[...] = jnp.zeros_like(o_ref)


def _moe_down_kernel(te_ref, nt_ref, a_ref, w_ref, b_ref, o_ref, wbf_ref):
    i = pl.program_id(1)
    active = i < nt_ref[0]

    @pl.when(active & _expert_changed(te_ref, i))
    def _():
        wbf_ref[...] = w_ref[...].astype(BF16)

    @pl.when(active)
    def _():
        o_ref[...] = jnp.dot(a_ref[...], wbf_ref[...], preferred_element_type=F32) + b_ref[...]

    @pl.when(jnp.logical_not(active))
    def _():
        o_ref[...] = jnp.zeros_like(o_ref)


def _moe_experts(x_sorted, tile_expert, n_used, w1, b1g, b1l, w2, b2, layer, tm):
    p, d = x_sorted.shape
    f = w2.shape[2]
    n_tiles = p // tm
    tf = _tile(f, 512)
    tn = _tile(d, 1024)

    def row(i, nt):
        return jnp.minimum(i, nt[0] - 1)

    up_spec = pltpu.PrefetchScalarGridSpec(
        num_scalar_prefetch=2,
        grid=(f // tf, n_tiles),
        in_specs=[pl.BlockSpec((tm, d), lambda c, i, te, nt: (row(i, nt), 0)),
                  pl.BlockSpec((None, None, d, 2 * tf), lambda c, i, te, nt: (layer, te[i], 0, c)),
                  pl.BlockSpec((None, None, 1, tf), lambda c, i, te, nt: (layer, te[i], 0, c)),
                  pl.BlockSpec((None, None, 1, tf), lambda c, i, te, nt: (layer, te[i], 0, c))],
        out_specs=pl.BlockSpec((tm, tf), lambda c, i, te, nt: (i, c)),
        scratch_shapes=[pltpu.VMEM((d, 2 * tf), BF16)],
    )
    act = pl.pallas_call(
        _moe_up_kernel,
        grid_spec=up_spec,
        out_shape=jax.ShapeDtypeStruct((p, f), BF16),
        compiler_params=_params(("arbitrary", "arbitrary"), 48),
        name="moe_up",
    )(tile_expert, n_used, x_sorted, w1, b1g, b1l)

    down_spec = pltpu.PrefetchScalarGridSpec(
        num_scalar_prefetch=2,
        grid=(d // tn, n_tiles),
        in_specs=[pl.BlockSpec((tm, f), lambda c, i, te, nt: (row(i, nt), 0)),
                  pl.BlockSpec((None, None, f, tn), lambda c, i, te, nt: (layer, te[i], 0, c)),
                  pl.BlockSpec((None, None, 1, tn), lambda c, i, te, nt: (layer, te[i], 0, c))],
        out_specs=pl.BlockSpec((tm, tn), lambda c, i, te, nt: (i, c)),
        scratch_shapes=[pltpu.VMEM((f, tn), BF16)],
    )
    return pl.pallas_call(
        _moe_down_kernel,
        grid_spec=down_spec,
        out_shape=jax.ShapeDtypeStruct((p, d), F32),
        compiler_params=_params(("arbitrary", "arbitrary"), 48),
        name="moe_down",
    )(tile_expert, n_used, act, w2, b2)


def _combine_kernel(*refs, tc, final_norm):
    if final_norm:
        cur_ref, nxt_ref, y_hbm, h_ref, w_ref, g_ref, fg_ref, o_ref, buf, sem = refs
    else:
        cur_ref, nxt_ref, y_hbm, h_ref, w_ref, g_ref, o_ref, buf, sem = refs
    i = pl.program_id(0)
    n = pl.num_programs(0)
    rows = TOP_K * tc

    def issue(idx_ref, slot):
        def body(r, c):
            _row_copy(y_hbm, idx_ref[0, 0, r], buf, slot, r, sem).start()
            return c
        lax.fori_loop(0, rows, body, 0)

    @pl.when(i == 0)
    def _():
        issue(cur_ref, 0)

    @pl.when(i + 1 < n)
    def _():
        issue(nxt_ref, (i + 1) % 2)

    slot = i % 2

    def wait_body(r, c):
        _row_copy(y_hbm, 0, buf, slot, r, sem).wait()
        return c
    lax.fori_loop(0, rows, wait_body, 0)

    w = w_ref[...]
    mix = w[:, 0:1] * buf[slot, 0:tc, :]
    for k in range(1, TOP_K):
        mix = mix + w[:, k:k + 1] * buf[slot, k * tc:(k + 1) * tc, :]
    out = h_ref[...] + g_ref[...] * mix
    if final_norm:
        out = _rms(out, fg_ref[...])
    o_ref[...] = out


def _moe_combine(y_sorted, pos, top_w, h, gate, seq, final_g=None):
    t, d = h.shape
    tc = _tile(seq, 128)
    n = t // tc
    per_batch = seq // tc
    pos3 = pos.reshape(n, tc, TOP_K).transpose(0, 2, 1).reshape(n, 1, TOP_K * tc)
    smem = functools.partial(pl.BlockSpec, memory_space=pltpu.SMEM)
    final_norm = final_g is not None
    in_specs = [smem((1, 1, TOP_K * tc), lambda i: (i, 0, 0)),
                smem((1, 1, TOP_K * tc), lambda i: (jnp.minimum(i + 1, n - 1), 0, 0)),
                pl.BlockSpec(memory_space=pl.ANY),
                pl.BlockSpec((tc, d), lambda i: (i, 0)),
                pl.BlockSpec((tc, TOP_K), lambda i: (i, 0)),
                pl.BlockSpec((None, 1, d), lambda i: (i // per_batch, 0, 0))]
    args = [pos3, pos3, y_sorted, h, top_w, gate]
    if final_norm:
        in_specs.append(pl.BlockSpec((1, d), lambda i: (0, 0)))
        args.append(final_g.reshape(1, d))
    return pl.pallas_call(
        functools.partial(_combine_kernel, tc=tc, final_norm=final_norm),
        grid=(n,),
        in_specs=in_specs,
        out_specs=pl.BlockSpec((tc, d), lambda i: (i, 0)),
        out_shape=jax.ShapeDtypeStruct((t, d), F32),
        scratch_shapes=[pltpu.VMEM((2, TOP_K * tc, d), F32), pltpu.SemaphoreType.DMA((2,))],
        compiler_params=_params(("arbitrary",), 32),
        name="moe_combine",
    )(*args)


def _moe_layer(h, layer, seq, norm_g, scale, shift, gate, router_w, router_b,
               moe_w1, moe_b1, moe_w2, moe_b2, final_g=None):
    n_experts = router_w.shape[-1]
    d = h.shape[1]
    lanes = V7X_LANES * pl.cdiv(n_experts, V7X_LANES)
    rw_pad = jnp.pad(router_w[layer], ((0, 0), (0, lanes - n_experts)))
    rb_pad = jnp.pad(router_b[layer], (0, lanes - n_experts),
                     constant_values=NEG_BIG).reshape(1, lanes)
    f_in, top_idx, top_w = _moe_route(h, norm_g, scale, shift, rw_pad, rb_pad, seq)
    tm = 256
    src_tok, pos, tile_expert, n_used = _routing_tables(top_idx, n_experts, tm)
    x_sorted = _moe_dispatch(f_in, src_tok, n_used, tm)
    f2 = moe_b1.shape[-1]
    b1 = moe_b1.reshape(moe_b1.shape[0], n_experts, f2 // 2, 2)
    b1g = b1[..., 0].reshape(moe_b1.shape[0], n_experts, 1, f2 // 2)
    b1l = b1[..., 1].reshape(moe_b1.shape[0], n_experts, 1, f2 // 2)
    b2 = moe_b2.reshape(moe_b2.shape[0], n_experts, 1, d)
    y_sorted = _moe_experts(x_sorted, tile_expert, n_used, moe_w1, b1g, b1l, moe_w2, b2,
                            layer, tm)
    return _moe_combine(y_sorted, pos, top_w, h, gate, seq, final_g)


def _mla_weight_layouts(w_in, w_q_up, w_kv_up, heads, q_lora, kv_lora):
    half = ROPE_DIM // 2
    pad = HEAD_DIM - ROPE_DIM
    d = w_in.shape[0]
    rope = w_in[:, q_lora + kv_lora:]
    w_in_p = jnp.concatenate(
        [w_in[:, :q_lora + kv_lora], rope[:, 0::2], rope[:, 1::2], jnp.zeros((d, pad), F32)],
        axis=1).astype(BF16)
    wq = w_q_up.reshape(q_lora, heads, HEAD_DIM + ROPE_DIM)
    wq_rope = wq[:, :, HEAD_DIM:]
    wq_p = jnp.concatenate(
        [wq[:, :, :HEAD_DIM], wq_rope[:, :, 0::2], wq_rope[:, :, 1::2],
         jnp.zeros((q_lora, heads, pad), F32)], axis=2)
    wq_p = wq_p.reshape(q_lora, heads * 2 * HEAD_DIM).astype(BF16)
    wkv = w_kv_up.reshape(kv_lora, heads, 2 * HEAD_DIM)
    wkv_p = jnp.concatenate([wkv[:, :, :HEAD_DIM].reshape(kv_lora, heads * HEAD_DIM),
                             wkv[:, :, HEAD_DIM:].reshape(kv_lora, heads * HEAD_DIM)],
                            axis=1).astype(BF16)
    assert half * 2 == ROPE_DIM and w_in_p.shape[1] == q_lora + kv_lora + HEAD_DIM
    return w_in_p, wq_p, wkv_p


def kernel(x, c, positions, mod_w, mod_b, attn_norm_g, ffn_norm_g, mla_w_in, mla_q_norm_g,
           mla_kv_norm_g, mla_w_q_up, mla_w_kv_up, mla_w_out, shared_norm_g, shared_w_kvf,
           shared_b_f, fox_w_q, fox_w_out, router_w, router_b, moe_w1, moe_b1, moe_w2, moe_b2,
           final_norm_g):
    b, s, d = x.shape
    depth = mod_w.shape[0]
    n_a = mla_w_in.shape[0]
    heads = d // HEAD_DIM
    q_lora = mla_q_norm_g.shape[-1]
    kv_lora = mla_kv_norm_g.shape[-1]
    hd = heads * HEAD_DIM
    t = b * s

    c_pad = jnp.pad(c, ((0, (-b) % V7X_SUBLANES), (0, 0)))
    mod = _modulation(c_pad, mod_w, mod_b)[:, :b, :]
    tables = _rope_tables(positions)

    h = x.reshape(t, d)
    kv_sh = fq = fk = None
    for layer in range(depth):
        sh_a, sc_a, g_a, sh_f, sc_f, g_f = (
            mod[layer, :, i * d:(i + 1) * d].reshape(b, 1, d) for i in range(N_MOD))
        if layer < n_a:
            w_in_p, wq_p, wkv_p = _mla_weight_layouts(
                mla_w_in[layer], mla_w_q_up[layer], mla_w_kv_up[layer], heads, q_lora, kv_lora)
            proj = _norm_matmul(h, attn_norm_g[layer], w_in_p, s, scale=sc_a, shift=sh_a)
            q, k, v = _mla_qkv(proj, mla_q_norm_g[layer], mla_kv_norm_g[layer], wq_p, wkv_p,
                               tables, heads, q_lora, kv_lora,
                               (HEAD_DIM + ROPE_DIM) ** -0.5)
            o = _flash_attention(q.reshape(b, s, -1), k.reshape(b, s, -1),
                                 v.reshape(b, s, -1), heads, HEAD_DIM)
            w_out = mla_w_out[layer].astype(BF16)
        else:
            j = layer - n_a
            q = _norm_matmul(h, attn_norm_g[layer], fox_w_q[j].astype(BF16), s, scale=sc_a,
                             shift=sh_a, out_dtype=BF16, out_scale=HEAD_DIM ** -0.5)
            o = _flash_attention(q.reshape(b, s, hd), kv_sh, kv_sh, heads, HEAD_DIM,
                                 v_block0=heads, fq=fq, fk=fk)
            w_out = fox_w_out[j].astype(BF16)
        h = _out_proj_residual(o.reshape(t, hd), w_out, h, g_a, s)
        last = layer == depth - 1
        h = _moe_layer(h, layer, s, ffn_norm_g[layer], sc_f, sh_f, g_f, router_w, router_b,
                       moe_w1, moe_b1, moe_w2, moe_b2, final_norm_g if last else None)
        if layer == n_a - 1:
            kv_sh = _norm_matmul(h, shared_norm_g, shared_w_kvf[:, :2 * hd].astype(BF16), s,
                                 out_dtype=BF16).reshape(b, s, 2 * hd)
            w_f = jnp.pad(shared_w_kvf[:, 2 * hd:], ((0, 0), (0, V7X_LANES - heads)))
            f_logit = _norm_matmul(h, shared_norm_g, w_f, s)
            b_pad = jnp.pad(shared_b_f, (0, V7X_LANES - heads)).reshape(1, V7X_LANES)
            fq = _forget_cumsum(f_logit.reshape(b, s, V7X_LANES), b_pad)
            fk = fq[:, :, :heads].transpose(0, 2, 1).reshape(b, heads, 1, s)
    return h.reshape(b, s, d)
```

```python
import functools

import jax
import jax.numpy as jnp
from jax import lax
from jax.experimental import pallas as pl
from jax.experimental.pallas import tpu as pltpu

F32 = jnp.float32
BF16 = jnp.bfloat16
HIGHEST = lax.Precision.HIGHEST

RMS_EPS = 1e-6
HEAD_DIM = 128
ROPE_DIM = 64
ROPE_THETA = 10000.0
TOP_K = 4
SWIGLU_ALPHA = 1.702
SWIGLU_LIMIT = 7.0
N_MOD = 6

V7X_LANES = 128
V7X_SUBLANES = 8
V7X_MXU_DIM = 256
V7X_VMEM_BYTES = 64 * 1024 * 1024
MIB = 1024 * 1024

NEG_BIG = -1e30


def _params(semantics, vmem_mib):
    assert vmem_mib * MIB < V7X_VMEM_BYTES
    return pltpu.CompilerParams(dimension_semantics=semantics,
                                vmem_limit_bytes=vmem_mib * MIB)


def _tile(n, pref):
    if n <= pref:
        return n
    t = pref
    while n % t:
        t //= 2
    return t


def _rms(x, g):
    ms = jnp.mean(x * x, axis=-1, keepdims=True)
    return x * lax.rsqrt(ms + RMS_EPS) * g


def _store_row_contiguous(ref, x):
    rows, d = x.shape
    lines = d // V7X_LANES
    for s in range(lines):
        ref[pl.ds(s, rows, stride=lines), :] = x[:, s * V7X_LANES:(s + 1) * V7X_LANES]


def _load_row_contiguous(ref, base, rows, lines):
    return jnp.concatenate(
        [ref[pl.ds(base + s, rows, stride=lines), :] for s in range(lines)], axis=1)


def _mod_kernel(c_ref, w_ref, b_ref, o_ref):
    o_ref[...] = jnp.dot(c_ref[...], w_ref[...], precision=HIGHEST,
                         preferred_element_type=F32) + b_ref[...]


def _modulation(c_pad, mod_w, mod_b):
    n_layers, d, n = mod_w.shape
    rows = c_pad.shape[0]
    tn = _tile(n, 1024)
    return pl.pallas_call(
        _mod_kernel,
        grid=(n_layers, n // tn),
        in_specs=[
            pl.BlockSpec((rows, d), lambda l, j: (0, 0)),
            pl.BlockSpec((None, d, tn), lambda l, j: (l, 0, j)),
            pl.BlockSpec((None, 1, tn), lambda l, j: (l, 0, j)),
        ],
        out_specs=pl.BlockSpec((None, rows, tn), lambda l, j: (l, 0, j)),
        out_shape=jax.ShapeDtypeStruct((n_layers, rows, n), F32),
        compiler_params=_params(("arbitrary", "arbitrary"), 40),
        name="modulation",
    )(c_pad, mod_w, mod_b.reshape(n_layers, 1, n))


def _norm_matmul_kernel(*refs, modulate, out_scale):
    if modulate:
        h_ref, g_ref, sc_ref, sh_ref, w_ref, o_ref = refs
    else:
        h_ref, g_ref, w_ref, o_ref = refs
    y = _rms(h_ref[...], g_ref[...])
    if modulate:
        y = y * (1.0 + sc_ref[...]) + sh_ref[...]
    w = w_ref[...]
    if w.dtype == BF16:
        out = jnp.dot(y.astype(BF16), w, preferred_element_type=F32)
    else:
        out = jnp.dot(y, w, precision=HIGHEST, preferred_element_type=F32)
    if out_scale != 1.0:
        out = out * out_scale
    o_ref[...] = out.astype(o_ref.dtype)


def _norm_matmul(h, g, w, seq, *, scale=None, shift=None, out_dtype=F32, out_scale=1.0):
    t, d = h.shape
    n = w.shape[1]
    tm = _tile(seq, 256)
    tn = _tile(n, 2048)
    per_batch = seq // tm
    modulate = scale is not None
    in_specs = [pl.BlockSpec((tm, d), lambda j, i: (i, 0)),
                pl.BlockSpec((1, d), lambda j, i: (0, 0))]
    args = [h, g.reshape(1, d)]
    if modulate:
        mod_spec = pl.BlockSpec((None, 1, d), lambda j, i: (i // per_batch, 0, 0))
        in_specs += [mod_spec, mod_spec]
        args += [scale, shift]
    in_specs.append(pl.BlockSpec((d, tn), lambda j, i: (0, j)))
    args.append(w)
    return pl.pallas_call(
        functools.partial(_norm_matmul_kernel, modulate=modulate, out_scale=out_scale),
        grid=(n // tn, t // tm),
        in_specs=in_specs,
        out_specs=pl.BlockSpec((tm, tn), lambda j, i: (i, j)),
        out_shape=jax.ShapeDtypeStruct((t, n), out_dtype),
        compiler_params=_params(("arbitrary", "arbitrary"), 48),
        name="norm_matmul",
    )(*args)


def _rope_table_kernel(pos_ref, freq_ref, c_ref, s1_ref, s2_ref):
    half = ROPE_DIM // 2
    ang = pos_ref[...].astype(F32) * freq_ref[...]
    lane = lax.broadcasted_iota(jnp.int32, ang.shape, 1)
    cos = jnp.cos(ang)
    sin = jnp.sin(ang)
    c_ref[...] = jnp.where(lane < 2 * half, cos, 0.0)
    s1_ref[...] = jnp.where(lane < half, -sin, 0.0)
    s2_ref[...] = jnp.where((lane >= half) & (lane < 2 * half), sin, 0.0)


def _rope_tables(positions):
    t = positions.size
    half = ROPE_DIM // 2
    inv_freq = ROPE_THETA ** (-jnp.arange(0, ROPE_DIM, 2, dtype=F32) / ROPE_DIM)
    freq = jnp.tile(inv_freq, V7X_LANES // half).reshape(1, V7X_LANES)
    tm = _tile(t, 512)
    spec = pl.BlockSpec((tm, V7X_LANES), lambda i: (i, 0))
    shape = jax.ShapeDtypeStruct((t, V7X_LANES), F32)
    return pl.pallas_call(
        _rope_table_kernel,
        grid=(t // tm,),
        in_specs=[pl.BlockSpec((tm, 1), lambda i: (i, 0)),
                  pl.BlockSpec((1, V7X_LANES), lambda i: (0, 0))],
        out_specs=[spec, spec, spec],
        out_shape=[shape, shape, shape],
        compiler_params=_params(("arbitrary",), 16),
        name="rope_tables",
    )(positions.reshape(t, 1), freq)


def _rope_chunk(z, c, s1, s2):
    half = ROPE_DIM // 2
    return (z * c + pltpu.roll(z, V7X_LANES - half, axis=1) * s1
            + pltpu.roll(z, half, axis=1) * s2)


def _mla_q_kernel(x_ref, g_ref, w_ref, c_ref, s1_ref, s2_ref, o_ref, *, heads, scale):
    y = _rms(x_ref[...], g_ref[...]).astype(BF16)
    q = jnp.dot(y, w_ref[...], preferred_element_type=F32) * scale
    c, s1, s2 = c_ref[...], s1_ref[...], s2_ref[...]
    for h in range(heads):
        base = h * 2 * HEAD_DIM
        o_ref[:, base:base + HEAD_DIM] = q[:, base:base + HEAD_DIM].astype(BF16)
        z = q[:, base + HEAD_DIM:base + 2 * HEAD_DIM]
        o_ref[:, base + HEAD_DIM:base + 2 * HEAD_DIM] = _rope_chunk(z, c, s1, s2).astype(BF16)


def _mla_kv_kernel(x_ref, r_ref, g_ref, w_ref, c_ref, s1_ref, s2_ref, k_ref, v_ref, *, heads):
    y = _rms(x_ref[...], g_ref[...]).astype(BF16)
    kv = jnp.dot(y, w_ref[...], preferred_element_type=F32)
    kr = _rope_chunk(r_ref[...], c_ref[...], s1_ref[...], s2_ref[...]).astype(BF16)
    for h in range(heads):
        base = h * 2 * HEAD_DIM
        k_ref[:, base:base + HEAD_DIM] = kv[:, h * HEAD_DIM:(h + 1) * HEAD_DIM].astype(BF16)
        k_ref[:, base + HEAD_DIM:base + 2 * HEAD_DIM] = kr
    v_ref[...] = kv[:, heads * HEAD_DIM:].astype(BF16)


def _mla_qkv(proj, q_norm_g, kv_norm_g, wq, wkv, tables, heads, q_lora, kv_lora, scale):
    t = proj.shape[0]
    tm = _tile(t, 256)
    dq = heads * 2 * HEAD_DIM
    assert q_lora == kv_lora and q_lora % V7X_LANES == 0
    tab_spec = pl.BlockSpec((tm, V7X_LANES), lambda i: (i, 0))
    rope_block = (q_lora + kv_lora) // V7X_LANES
    q = pl.pallas_call(
        functools.partial(_mla_q_kernel, heads=heads, scale=scale),
        grid=(t // tm,),
        in_specs=[pl.BlockSpec((tm, q_lora), lambda i: (i, 0)),
                  pl.BlockSpec((1, q_lora), lambda i: (0, 0)),
                  pl.BlockSpec((q_lora, dq), lambda i: (0, 0)),
                  tab_spec, tab_spec, tab_spec],
        out_specs=pl.BlockSpec((tm, dq), lambda i: (i, 0)),
        out_shape=jax.ShapeDtypeStruct((t, dq), BF16),
        compiler_params=_params(("arbitrary",), 40),
        name="mla_q",
    )(proj, q_norm_g.reshape(1, q_lora), wq, *tables)
    k, v = pl.pallas_call(
        functools.partial(_mla_kv_kernel, heads=heads),
        grid=(t // tm,),
        in_specs=[pl.BlockSpec((tm, kv_lora), lambda i: (i, 1)),
                  pl.BlockSpec((tm, V7X_LANES), lambda i: (i, rope_block)),
                  pl.BlockSpec((1, kv_lora), lambda i: (0, 0)),
                  pl.BlockSpec((kv_lora, dq), lambda i: (0, 0)),
                  tab_spec, tab_spec, tab_spec],
        out_specs=[pl.BlockSpec((tm, dq), lambda i: (i, 0)),
                   pl.BlockSpec((tm, heads * HEAD_DIM), lambda i: (i, 0))],
        out_shape=[jax.ShapeDtypeStruct((t, dq), BF16),
                   jax.ShapeDtypeStruct((t, heads * HEAD_DIM), BF16)],
        compiler_params=_params(("arbitrary",), 40),
        name="mla_kv",
    )(proj, proj, kv_norm_g.reshape(1, kv_lora), wkv, *tables)
    return q, k, v


FLASH_HEADS_PER_STEP = 2


def _flash_kernel(*refs, tq, dk, dv, hp, decay):
    if decay:
        q_ref, k_ref, v_ref, fq_ref, fk_ref, o_ref = refs
    else:
        q_ref, k_ref, v_ref, o_ref = refs
    qi = pl.program_id(2)
    qs = [q_ref[:, a * dk:(a + 1) * dk] for a in range(hp)]
    if decay:
        fq_all = fq_ref[...]
        lane = lax.broadcasted_iota(jnp.int32, fq_all.shape, 1)
        fqs = [jnp.sum(jnp.where(lane == pl.program_id(1) * hp + a, fq_all, 0.0), axis=-1,
                       keepdims=True) for a in range(hp)]

    def block(j, carry, masked):
        start = pl.multiple_of(j * tq, tq)
        out = []
        for a in range(hp):
            m, l, acc = carry[a]
            k = k_ref[pl.ds(start, tq), a * dk:(a + 1) * dk]
            v = v_ref[pl.ds(start, tq), a * dv:(a + 1) * dv]
            s = lax.dot_general(qs[a], k, (((1,), (1,)), ((), ())),
                                preferred_element_type=F32)
            if decay:
                s = s + (fqs[a] - fk_ref[a, :, pl.ds(start, tq)])
            if masked:
                row = lax.broadcasted_iota(jnp.int32, s.shape, 0)
                col = lax.broadcasted_iota(jnp.int32, s.shape, 1)
                s = jnp.where(col <= row, s, -jnp.inf)
            m_new = jnp.maximum(m, jnp.max(s, axis=-1, keepdims=True))
            p = jnp.exp(s - m_new)
            alpha = jnp.exp(m - m_new)
            l = alpha * l + jnp.sum(p, axis=-1, keepdims=True)
            acc = alpha * acc + jnp.dot(p.astype(BF16), v, preferred_element_type=F32)
            out.append((m_new, l, acc))
        return tuple(out)

    init = tuple((jnp.full((tq, 1), -jnp.inf, F32), jnp.zeros((tq, 1), F32),
                  jnp.zeros((tq, dv), F32)) for _ in range(hp))
    carry = lax.fori_loop(0, qi, lambda j, c: block(j, c, False), init)
    final = block(qi, carry, True)
    for a in range(hp):
        _, l, acc = final[a]
        o_ref[:, a * dv:(a + 1) * dv] = (acc / l).astype(o_ref.dtype)


def _flash_attention(q, k, v, heads, dv, v_block0=0, fq=None, fk=None):
    b, s, _ = q.shape
    dk = q.shape[-1] // heads
    tq = _tile(s, 512)
    hp = FLASH_HEADS_PER_STEP
    assert heads % hp == 0 and v_block0 % hp == 0
    vb0 = v_block0 // hp
    decay = fq is not None
    in_specs = [pl.BlockSpec((None, tq, hp * dk), lambda bi, h, i: (bi, i, h)),
                pl.BlockSpec((None, s, hp * dk), lambda bi, h, i: (bi, 0, h)),
                pl.BlockSpec((None, s, hp * dv), lambda bi, h, i: (bi, 0, vb0 + h))]
    args = [q, k, v]
    if decay:
        in_specs += [pl.BlockSpec((None, tq, V7X_LANES), lambda bi, h, i: (bi, i, 0)),
                     pl.BlockSpec((None, hp, 1, s), lambda bi, h, i: (bi, h, 0, 0))]
        args += [fq, fk]
    return pl.pallas_call(
        functools.partial(_flash_kernel, tq=tq, dk=dk, dv=dv, hp=hp, decay=decay),
        grid=(b, heads // hp, s // tq),
        in_specs=in_specs,
        out_specs=pl.BlockSpec((None, tq, hp * dv), lambda bi, h, i: (bi, i, h)),
        out_shape=jax.ShapeDtypeStruct((b, s, heads * dv), BF16),
        compiler_params=_params(("arbitrary", "arbitrary", "arbitrary"), 40),
        name="flash_attention",
    )(*args)


def _out_proj_kernel(o_ref, w_ref, h_ref, g_ref, out_ref):
    a = jnp.dot(o_ref[...], w_ref[...], preferred_element_type=F32)
    out_ref[...] = h_ref[...] + g_ref[...] * a


def _out_proj_residual(o, w, h, gate, seq):
    t, dk = o.shape
    d = w.shape[1]
    tm = _tile(seq, 256)
    per_batch = seq // tm
    return pl.pallas_call(
        _out_proj_kernel,
        grid=(t // tm,),
        in_specs=[pl.BlockSpec((tm, dk), lambda i: (i, 0)),
                  pl.BlockSpec((dk, d), lambda i: (0, 0)),
                  pl.BlockSpec((tm, d), lambda i: (i, 0)),
                  pl.BlockSpec((None, 1, d), lambda i: (i // per_batch, 0, 0))],
        out_specs=pl.BlockSpec((tm, d), lambda i: (i, 0)),
        out_shape=jax.ShapeDtypeStruct((t, d), F32),
        compiler_params=_params(("arbitrary",), 40),
        name="out_proj_residual",
    )(o, w, h, gate)


def _forget_cumsum_kernel(f_ref, b_ref, o_ref, carry_ref):
    @pl.when(pl.program_id(1) == 0)
    def _():
        carry_ref[...] = jnp.zeros_like(carry_ref)

    x = f_ref[...] + b_ref[...]
    log_f = jnp.minimum(x, 0.0) - jnp.log1p(jnp.exp(-jnp.abs(x)))
    ts = x.shape[0]
    row = lax.broadcasted_iota(jnp.int32, (ts, ts), 0)
    col = lax.broadcasted_iota(jnp.int32, (ts, ts), 1)
    tri = jnp.where(col <= row, 1.0, 0.0).astype(F32)
    c = jnp.dot(tri, log_f, precision=HIGHEST, preferred_element_type=F32) + carry_ref[...]
    o_ref[...] = c
    carry_ref[...] = c[ts - 1:ts, :]


def _forget_cumsum(f_logit, b_pad):
    b, s, lanes = f_logit.shape
    ts = _tile(s, 512)
    return pl.pallas_call(
        _forget_cumsum_kernel,
        grid=(b, s // ts),
        in_specs=[pl.BlockSpec((None, ts, lanes), lambda bi, i: (bi, i, 0)),
                  pl.BlockSpec((1, lanes), lambda bi, i: (0, 0))],
        out_specs=pl.BlockSpec((None, ts, lanes), lambda bi, i: (bi, i, 0)),
        out_shape=jax.ShapeDtypeStruct((b, s, lanes), F32),
        scratch_shapes=[pltpu.VMEM((1, lanes), F32)],
        compiler_params=_params(("arbitrary", "arbitrary"), 16),
        name="forget_cumsum",
    )(f_logit, b_pad)


def _moe_route_kernel(h_ref, g_ref, sc_ref, sh_ref, rw_ref, rb_ref, f_ref, idx_ref, w_ref):
    y = _rms(h_ref[...], g_ref[...]) * (1.0 + sc_ref[...]) + sh_ref[...]
    _store_row_contiguous(f_ref, y)
    logits = jnp.dot(y, rw_ref[...], precision=HIGHEST,
                     preferred_element_type=F32) + rb_ref[...]
    lane = lax.broadcasted_iota(jnp.int32, logits.shape, 1).astype(F32)
    vals, idxs = [], []
    for _ in range(TOP_K):
        m = jnp.max(logits, axis=-1, keepdims=True)
        ix = jnp.min(jnp.where(logits == m, lane, float(logits.shape[1])), axis=-1,
                     keepdims=True)
        vals.append(m)
        idxs.append(ix)
        logits = jnp.where(lane == ix, -jnp.inf, logits)
    exps = [jnp.exp(v - vals[0]) for v in vals]
    den = exps[0]
    for e in exps[1:]:
        den = den + e
    slot = lax.broadcasted_iota(jnp.int32, idx_ref.shape, 1)
    idx_out = jnp.zeros(idx_ref.shape, jnp.int32)
    w_out = jnp.zeros(w_ref.shape, F32)
    for k in range(TOP_K):
        idx_out = jnp.where(slot == k, idxs[k].astype(jnp.int32), idx_out)
        w_out = jnp.where(slot == k, exps[k] / den, w_out)
    idx_ref[...] = idx_out
    w_ref[...] = w_out


def _moe_route(h, g, scale, shift, rw_pad, rb_pad, seq):
    t, d = h.shape
    tm = _tile(seq, 256)
    per_batch = seq // tm
    lanes = rw_pad.shape[1]
    lines = d // V7X_LANES
    mod_spec = pl.BlockSpec((None, 1, d), lambda i: (i // per_batch, 0, 0))
    return pl.pallas_call(
        _moe_route_kernel,
        grid=(t // tm,),
        in_specs=[pl.BlockSpec((tm, d), lambda i: (i, 0)),
                  pl.BlockSpec((1, d), lambda i: (0, 0)),
                  mod_spec, mod_spec,
                  pl.BlockSpec((d, lanes), lambda i: (0, 0)),
                  pl.BlockSpec((1, lanes), lambda i: (0, 0))],
        out_specs=[pl.BlockSpec((tm * lines, V7X_LANES), lambda i: (i, 0)),
                   pl.BlockSpec((tm, TOP_K), lambda i: (i, 0)),
                   pl.BlockSpec((tm, TOP_K), lambda i: (i, 0))],
        out_shape=[jax.ShapeDtypeStruct((t * lines, V7X_LANES), F32),
                   jax.ShapeDtypeStruct((t, TOP_K), jnp.int32),
                   jax.ShapeDtypeStruct((t, TOP_K), F32)],
        compiler_params=_params(("arbitrary",), 32),
        name="moe_route",
    )(h, g.reshape(1, d), scale, shift, rw_pad, rb_pad)


def _routing_tables(top_idx, n_experts, tm):
    t, k = top_idx.shape
    n_tiles = (t * k) // tm + n_experts
    e_flat = top_idx.reshape(-1)
    onehot = (e_flat[:, None] == jnp.arange(n_experts, dtype=jnp.int32)[None, :]).astype(jnp.int32)
    incl = jnp.cumsum(onehot, axis=0)
    counts = incl[-1]
    rank = jnp.sum((incl - onehot) * onehot, axis=1)
    tiles_per = (counts + tm - 1) // tm
    tile_end = jnp.cumsum(tiles_per)
    row_start = (tile_end - tiles_per) * tm
    pos = row_start[e_flat] + rank
    n_used = tile_end[-1]
    tile_ids = jnp.minimum(jnp.arange(n_tiles, dtype=jnp.int32), n_used - 1)
    tile_expert = jnp.sum((tile_ids[:, None] >= tile_end[None, :]).astype(jnp.int32), axis=1)
    tok = jnp.arange(t * k, dtype=jnp.int32) // k
    src_tok = jnp.zeros((n_tiles * tm,), jnp.int32).at[pos].set(tok)
    return (src_tok, pos.reshape(t, k), tile_expert.astype(jnp.int32),
            n_used.reshape(1).astype(jnp.int32))


ROW_DMA_UNROLL = 8


def _row_copy(src_hbm, row, buf, slot, r, sem):
    lines = src_hbm.shape[1]
    dst = buf.at[slot, pl.ds(pl.multiple_of(r * lines, lines), lines), :]
    return pltpu.make_async_copy(src_hbm.at[row], dst, sem.at[slot])


def _start_rows(src_hbm, idx_ref, n, buf, slot, sem):
    def body(r, c):
        _row_copy(src_hbm, idx_ref[0, 0, r], buf, slot, r, sem).start()
        return c
    lax.fori_loop(0, n, body, 0, unroll=ROW_DMA_UNROLL)


def _wait_rows(src_hbm, n, buf, slot, sem):
    def body(r, c):
        _row_copy(src_hbm, 0, buf, slot, r, sem).wait()
        return c
    lax.fori_loop(0, n, body, 0, unroll=ROW_DMA_UNROLL)


def _dispatch_kernel(nt_ref, cur_ref, nxt_ref, f_hbm, o_ref, buf, sem, *, tm):
    i = pl.program_id(0)
    nt = nt_ref[0]

    @pl.when(i == 0)
    def _():
        _start_rows(f_hbm, cur_ref, tm, buf, 0, sem)

    @pl.when(i + 1 < nt)
    def _():
        _start_rows(f_hbm, nxt_ref, tm, buf, (i + 1) % 2, sem)

    @pl.when(i < nt)
    def _():
        slot = i % 2
        _wait_rows(f_hbm, tm, buf, slot, sem)
        x = _load_row_contiguous(buf.at[slot], 0, tm, f_hbm.shape[1])
        o_ref[...] = x.astype(o_ref.dtype)

    @pl.when(i >= nt)
    def _():
        o_ref[...] = jnp.zeros_like(o_ref)


def _moe_dispatch(f_rows, src_tok, n_used, tm):
    t, lines, lanes = f_rows.shape
    d = lines * lanes
    n_tiles = src_tok.shape[0] // tm
    src3 = src_tok.reshape(n_tiles, 1, tm)
    smem = functools.partial(pl.BlockSpec, memory_space=pltpu.SMEM)
    grid_spec = pltpu.PrefetchScalarGridSpec(
        num_scalar_prefetch=1,
        grid=(n_tiles,),
        in_specs=[smem((1, 1, tm), lambda i, nt: (i, 0, 0)),
                  smem((1, 1, tm), lambda i, nt: (jnp.minimum(i + 1, n_tiles - 1), 0, 0)),
                  pl.BlockSpec(memory_space=pl.ANY)],
        out_specs=pl.BlockSpec((tm, d), lambda i, nt: (i, 0)),
        scratch_shapes=[pltpu.VMEM((2, tm * lines, lanes), F32),
                        pltpu.SemaphoreType.DMA((2,))],
    )
    return pl.pallas_call(
        functools.partial(_dispatch_kernel, tm=tm),
        grid_spec=grid_spec,
        out_shape=jax.ShapeDtypeStruct((n_tiles * tm, d), BF16),
        compiler_params=_params(("arbitrary",), 32),
        name="moe_dispatch",
    )(n_used, src3, src3, f_rows)


def _expert_changed(te_ref, i):
    return (i == 0) | (te_ref[i] != te_ref[jnp.maximum(i - 1, 0)])


def _moe_up_kernel(te_ref, nt_ref, x_ref, w_ref, bg_ref, bl_ref, o_ref, wbf_ref):
    i = pl.program_id(1)
    active = i < nt_ref[0]
    half = wbf_ref.shape[1] // 2
    group = V7X_MXU_DIM

    @pl.when(active & _expert_changed(te_ref, i))
    def _():
        src = lax.broadcasted_iota(jnp.int32, (group, group), 0)
        dst = lax.broadcasted_iota(jnp.int32, (group, group), 1)
        want = jnp.where(dst < group // 2, 2 * dst, 2 * (dst - group // 2) + 1)
        perm = jnp.where(src == want, 1.0, 0.0).astype(BF16)
        for gi in range(wbf_ref.shape[1] // group):
            blk = w_ref[:, gi * group:(gi + 1) * group].astype(BF16)
            split = jnp.dot(blk, perm, preferred_element_type=F32).astype(BF16)
            lo = gi * (group // 2)
            wbf_ref[:, lo:lo + group // 2] = split[:, :group // 2]
            wbf_ref[:, half + lo:half + lo + group // 2] = split[:, group // 2:]

    @pl.when(active)
    def _():
        u = jnp.dot(x_ref[...], wbf_ref[...], preferred_element_type=F32)
        glu = jnp.minimum(u[:, :half] + bg_ref[...], SWIGLU_LIMIT)
        lin = jnp.clip(u[:, half:] + bl_ref[...], -SWIGLU_LIMIT, SWIGLU_LIMIT)
        act = glu * jax.nn.sigmoid(SWIGLU_ALPHA * glu) * (lin + 1.0)
        o_ref[...] = act.astype(o_ref.dtype)

    @pl.when(jnp.logical_not(active))
    def _():
        o_ref[...] = jnp.zeros_like(o_ref)


def _moe_down_kernel(te_ref, nt_ref, a_ref, w_ref, b_ref, o_ref, wbf_ref):
    i = pl.program_id(0)
    active = i < nt_ref[0]

    @pl.when(active & _expert_changed(te_ref, i))
    def _():
        wbf_ref[...] = w_ref[...].astype(BF16)

    @pl.when(active)
    def _():
        y = jnp.dot(a_ref[...], wbf_ref[...], preferred_element_type=F32) + b_ref[...]
        _store_row_contiguous(o_ref, y)

    @pl.when(jnp.logical_not(active))
    def _():
        o_ref[...] = jnp.zeros_like(o_ref)


def _moe_experts(x_sorted, tile_expert, n_used, w1, b1g, b1l, w2, b2, layer, tm):
    p, d = x_sorted.shape
    f = w2.shape[2]
    n_tiles = p // tm
    tf = _tile(f, 1024)
    lines = d // V7X_LANES

    def row(i, nt):
        return jnp.minimum(i, nt[0] - 1)

    up_spec = pltpu.PrefetchScalarGridSpec(
        num_scalar_prefetch=2,
        grid=(f // tf, n_tiles),
        in_specs=[pl.BlockSpec((tm, d), lambda c, i, te, nt: (row(i, nt), 0)),
                  pl.BlockSpec((None, None, d, 2 * tf), lambda c, i, te, nt: (layer, te[i], 0, c)),
                  pl.BlockSpec((None, None, 1, tf), lambda c, i, te, nt: (layer, te[i], 0, c)),
                  pl.BlockSpec((None, None, 1, tf), lambda c, i, te, nt: (layer, te[i], 0, c))],
        out_specs=pl.BlockSpec((tm, tf), lambda c, i, te, nt: (i, c)),
        scratch_shapes=[pltpu.VMEM((d, 2 * tf), BF16)],
    )
    act = pl.pallas_call(
        _moe_up_kernel,
        grid_spec=up_spec,
        out_shape=jax.ShapeDtypeStruct((p, f), BF16),
        compiler_params=_params(("arbitrary", "arbitrary"), 56),
        name="moe_up",
    )(tile_expert, n_used, x_sorted, w1, b1g, b1l)

    down_spec = pltpu.PrefetchScalarGridSpec(
        num_scalar_prefetch=2,
        grid=(n_tiles,),
        in_specs=[pl.BlockSpec((tm, f), lambda i, te, nt: (row(i, nt), 0)),
                  pl.BlockSpec((None, None, f, d), lambda i, te, nt: (layer, te[i], 0, 0)),
                  pl.BlockSpec((None, None, 1, d), lambda i, te, nt: (layer, te[i], 0, 0))],
        out_specs=pl.BlockSpec((tm * lines, V7X_LANES), lambda i, te, nt: (i, 0)),
        scratch_shapes=[pltpu.VMEM((f, d), BF16)],
    )
    return pl.pallas_call(
        _moe_down_kernel,
        grid_spec=down_spec,
        out_shape=jax.ShapeDtypeStruct((p * lines, V7X_LANES), F32),
        compiler_params=_params(("arbitrary",), 56),
        name="moe_down",
    )(tile_expert, n_used, act, w2, b2)


def _combine_kernel(*refs, tc, final_norm):
    if final_norm:
        cur_ref, nxt_ref, y_hbm, h_ref, w_ref, g_ref, fg_ref, o_ref, buf, mix_ref, sem = refs
    else:
        cur_ref, nxt_ref, y_hbm, h_ref, w_ref, g_ref, o_ref, buf, mix_ref, sem = refs
    i = pl.program_id(0)
    n = pl.num_programs(0)
    rows = TOP_K * tc
    lines = y_hbm.shape[1]

    @pl.when(i == 0)
    def _():
        _start_rows(y_hbm, cur_ref, rows, buf, 0, sem)

    @pl.when(i + 1 < n)
    def _():
        _start_rows(y_hbm, nxt_ref, rows, buf, (i + 1) % 2, sem)

    slot = i % 2
    _wait_rows(y_hbm, rows, buf, slot, sem)

    w = w_ref[...]
    blk = tc * lines
    mix = w[:, 0:1] * buf[slot, pl.ds(0, blk), :]
    for k in range(1, TOP_K):
        mix = mix + w[:, k:k + 1] * buf[slot, pl.ds(k * blk, blk), :]
    mix_ref[...] = mix
    out = h_ref[...] + g_ref[...] * _load_row_contiguous(mix_ref, 0, tc, lines)
    if final_norm:
        out = _rms(out, fg_ref[...])
    o_ref[...] = out


def _moe_combine(y_rows, pos, top_w, h, gate, seq, final_g=None):
    t, d = h.shape
    lines = d // V7X_LANES
    tc = _tile(seq, 128)
    n = t // tc
    per_batch = seq // tc
    pos3 = pos.reshape(n, tc, TOP_K).transpose(0, 2, 1).reshape(n, 1, TOP_K * tc)
    w_lines = jnp.repeat(top_w, lines, axis=0)
    smem = functools.partial(pl.BlockSpec, memory_space=pltpu.SMEM)
    final_norm = final_g is not None
    in_specs = [smem((1, 1, TOP_K * tc), lambda i: (i, 0, 0)),
                smem((1, 1, TOP_K * tc), lambda i: (jnp.minimum(i + 1, n - 1), 0, 0)),
                pl.BlockSpec(memory_space=pl.ANY),
                pl.BlockSpec((tc, d), lambda i: (i, 0)),
                pl.BlockSpec((tc * lines, TOP_K), lambda i: (i, 0)),
                pl.BlockSpec((None, 1, d), lambda i: (i // per_batch, 0, 0))]
    args = [pos3, pos3, y_rows, h, w_lines, gate]
    if final_norm:
        in_specs.append(pl.BlockSpec((1, d), lambda i: (0, 0)))
        args.append(final_g.reshape(1, d))
    return pl.pallas_call(
        functools.partial(_combine_kernel, tc=tc, final_norm=final_norm),
        grid=(n,),
        in_specs=in_specs,
        out_specs=pl.BlockSpec((tc, d), lambda i: (i, 0)),
        out_shape=jax.ShapeDtypeStruct((t, d), F32),
        scratch_shapes=[pltpu.VMEM((2, TOP_K * tc * lines, V7X_LANES), F32),
                        pltpu.VMEM((tc * lines, V7X_LANES), F32),
                        pltpu.SemaphoreType.DMA((2,))],
        compiler_params=_params(("arbitrary",), 32),
        name="moe_combine",
    )(*args)


def _moe_layer(h, layer, seq, norm_g, scale, shift, gate, router_w, router_b,
               moe_w1, moe_b1, moe_w2, moe_b2, final_g=None):
    n_experts = router_w.shape[-1]
    d = h.shape[1]
    lanes = V7X_LANES * pl.cdiv(n_experts, V7X_LANES)
    rw_pad = jnp.pad(router_w[layer], ((0, 0), (0, lanes - n_experts)))
    rb_pad = jnp.pad(router_b[layer], (0, lanes - n_experts),
                     constant_values=NEG_BIG).reshape(1, lanes)
    f_in, top_idx, top_w = _moe_route(h, norm_g, scale, shift, rw_pad, rb_pad, seq)
    tm = 256
    src_tok, pos, tile_expert, n_used = _routing_tables(top_idx, n_experts, tm)
    lines = d // V7X_LANES
    x_sorted = _moe_dispatch(f_in.reshape(-1, lines, V7X_LANES), src_tok, n_used, tm)
    f2 = moe_b1.shape[-1]
    b1 = moe_b1.reshape(moe_b1.shape[0], n_experts, f2 // 2, 2)
    b1g = b1[..., 0].reshape(moe_b1.shape[0], n_experts, 1, f2 // 2)
    b1l = b1[..., 1].reshape(moe_b1.shape[0], n_experts, 1, f2 // 2)
    b2 = moe_b2.reshape(moe_b2.shape[0], n_experts, 1, d)
    y_sorted = _moe_experts(x_sorted, tile_expert, n_used, moe_w1, b1g, b1l, moe_w2, b2,
                            layer, tm)
    return _moe_combine(y_sorted.reshape(-1, lines, V7X_LANES), pos, top_w, h, gate, seq,
                        final_g)


def _mla_weight_layouts(w_in, w_q_up, w_kv_up, heads, q_lora, kv_lora):
    half = ROPE_DIM // 2
    pad = HEAD_DIM - ROPE_DIM
    d = w_in.shape[0]
    rope = w_in[:, q_lora + kv_lora:]
    w_in_p = jnp.concatenate(
        [w_in[:, :q_lora + kv_lora], rope[:, 0::2], rope[:, 1::2], jnp.zeros((d, pad), F32)],
        axis=1).astype(BF16)
    wq = w_q_up.reshape(q_lora, heads, HEAD_DIM + ROPE_DIM)
    wq_rope = wq[:, :, HEAD_DIM:]
    wq_p = jnp.concatenate(
        [wq[:, :, :HEAD_DIM], wq_rope[:, :, 0::2], wq_rope[:, :, 1::2],
         jnp.zeros((q_lora, heads, pad), F32)], axis=2)
    wq_p = wq_p.reshape(q_lora, heads * 2 * HEAD_DIM).astype(BF16)
    wkv = w_kv_up.reshape(kv_lora, heads, 2 * HEAD_DIM)
    wkv_p = jnp.concatenate([wkv[:, :, :HEAD_DIM].reshape(kv_lora, heads * HEAD_DIM),
                             wkv[:, :, HEAD_DIM:].reshape(kv_lora, heads * HEAD_DIM)],
                            axis=1).astype(BF16)
    assert half * 2 == ROPE_DIM and w_in_p.shape[1] == q_lora + kv_lora + HEAD_DIM
    return w_in_p, wq_p, wkv_p


def kernel(x, c, positions, mod_w, mod_b, attn_norm_g, ffn_norm_g, mla_w_in, mla_q_norm_g,
           mla_kv_norm_g, mla_w_q_up, mla_w_kv_up, mla_w_out, shared_norm_g, shared_w_kvf,
           shared_b_f, fox_w_q, fox_w_out, router_w, router_b, moe_w1, moe_b1, moe_w2, moe_b2,
           final_norm_g):
    b, s, d = x.shape
    depth = mod_w.shape[0]
    n_a = mla_w_in.shape[0]
    heads = d // HEAD_DIM
    q_lora = mla_q_norm_g.shape[-1]
    kv_lora = mla_kv_norm_g.shape[-1]
    hd = heads * HEAD_DIM
    t = b * s

    c_pad = jnp.pad(c, ((0, (-b) % V7X_SUBLANES), (0, 0)))
    mod = _modulation(c_pad, mod_w, mod_b)[:, :b, :]
    tables = _rope_tables(positions)

    h = x.reshape(t, d)
    kv_sh = fq = fk = None
    for layer in range(depth):
        sh_a, sc_a, g_a, sh_f, sc_f, g_f = (
            mod[layer, :, i * d:(i + 1) * d].reshape(b, 1, d) for i in range(N_MOD))
        if layer < n_a:
            w_in_p, wq_p, wkv_p = _mla_weight_layouts(
                mla_w_in[layer], mla_w_q_up[layer], mla_w_kv_up[layer], heads, q_lora, kv_lora)
            proj = _norm_matmul(h, attn_norm_g[layer], w_in_p, s, scale=sc_a, shift=sh_a)
            q, k, v = _mla_qkv(proj, mla_q_norm_g[layer], mla_kv_norm_g[layer], wq_p, wkv_p,
                               tables, heads, q_lora, kv_lora,
                               (HEAD_DIM + ROPE_DIM) ** -0.5)
            o = _flash_attention(q.reshape(b, s, -1), k.reshape(b, s, -1),
                                 v.reshape(b, s, -1), heads, HEAD_DIM)
            w_out = mla_w_out[layer].astype(BF16)
        else:
            j = layer - n_a
            q = _norm_matmul(h, attn_norm_g[layer], fox_w_q[j].astype(BF16), s, scale=sc_a,
                             shift=sh_a, out_dtype=BF16, out_scale=HEAD_DIM ** -0.5)
            o = _flash_attention(q.reshape(b, s, hd), kv_sh, kv_sh, heads, HEAD_DIM,
                                 v_block0=heads, fq=fq, fk=fk)
            w_out = fox_w_out[j].astype(BF16)
        h = _out_proj_residual(o.reshape(t, hd), w_out, h, g_a, s)
        last = layer == depth - 1
        h = _moe_layer(h, layer, s, ffn_norm_g[layer], sc_f, sh_f, g_f, router_w, router_b,
                       moe_w1, moe_b1, moe_w2, moe_b2, final_norm_g if last else None)
        if layer == n_a - 1:
            kv_sh = _norm_matmul(h, shared_norm_g, shared_w_kvf[:, :2 * hd].astype(BF16), s,
                                 out_dtype=BF16).reshape(b, s, 2 * hd)
            w_f = jnp.pad(shared_w_kvf[:, 2 * hd:], ((0, 0), (0, V7X_LANES - heads)))
            f_logit = _norm_matmul(h, shared_norm_g, w_f, s)
            b_pad = jnp.pad(shared_b_f, (0, V7X_LANES - heads)).reshape(1, V7X_LANES)
            fq = _forget_cumsum(f_logit.reshape(b, s, V7X_LANES), b_pad)
            fk = fq[:, :, :heads].transpose(0, 2, 1).reshape(b, heads, 1, s)
    return h.reshape(b, s, d)
```

```python
import functools

import jax
import jax.numpy as jnp
from jax import lax
from jax.experimental import pallas as pl
from jax.experimental.pallas import tpu as pltpu

F32 = jnp.float32
BF16 = jnp.bfloat16
HIGHEST = lax.Precision.HIGHEST

RMS_EPS = 1e-6
HEAD_DIM = 128
ROPE_DIM = 64
ROPE_THETA = 10000.0
TOP_K = 4
SWIGLU_ALPHA = 1.702
SWIGLU_LIMIT = 7.0
N_MOD = 6

V7X_LANES = 128
V7X_SUBLANES = 8
V7X_MXU_DIM = 256
V7X_VMEM_BYTES = 64 * 1024 * 1024
MIB = 1024 * 1024

NEG_BIG = -1e30


def _params(semantics, vmem_mib):
    assert vmem_mib * MIB < V7X_VMEM_BYTES
    return pltpu.CompilerParams(dimension_semantics=semantics,
                                vmem_limit_bytes=vmem_mib * MIB)


def _tile(n, pref):
    if n <= pref:
        return n
    t = pref
    while n % t:
        t //= 2
    return t


def _rms(x, g):
    ms = jnp.mean(x * x, axis=-1, keepdims=True)
    return x * lax.rsqrt(ms + RMS_EPS) * g


def _store_row_contiguous(ref, x):
    rows, d = x.shape
    lines = d // V7X_LANES
    for s in range(lines):
        ref[pl.ds(s, rows, stride=lines), :] = x[:, s * V7X_LANES:(s + 1) * V7X_LANES]


def _load_row_contiguous(ref, base, rows, lines):
    return jnp.concatenate(
        [ref[pl.ds(base + s, rows, stride=lines), :] for s in range(lines)], axis=1)


def _mod_kernel(c_ref, w_ref, b_ref, o_ref):
    o_ref[...] = jnp.dot(c_ref[...], w_ref[...], precision=HIGHEST,
                         preferred_element_type=F32) + b_ref[...]


def _modulation(c_pad, mod_w, mod_b):
    n_layers, d, n = mod_w.shape
    rows = c_pad.shape[0]
    tn = _tile(n, 1024)
    return pl.pallas_call(
        _mod_kernel,
        grid=(n_layers, n // tn),
        in_specs=[
            pl.BlockSpec((rows, d), lambda l, j: (0, 0)),
            pl.BlockSpec((None, d, tn), lambda l, j: (l, 0, j)),
            pl.BlockSpec((None, 1, tn), lambda l, j: (l, 0, j)),
        ],
        out_specs=pl.BlockSpec((None, rows, tn), lambda l, j: (l, 0, j)),
        out_shape=jax.ShapeDtypeStruct((n_layers, rows, n), F32),
        compiler_params=_params(("arbitrary", "arbitrary"), 40),
        name="modulation",
    )(c_pad, mod_w, mod_b.reshape(n_layers, 1, n))


def _norm_matmul_kernel(*refs, modulate, out_scale):
    if modulate:
        h_ref, g_ref, sc_ref, sh_ref, w_ref, o_ref = refs
    else:
        h_ref, g_ref, w_ref, o_ref = refs
    y = _rms(h_ref[...], g_ref[...])
    if modulate:
        y = y * (1.0 + sc_ref[...]) + sh_ref[...]
    w = w_ref[...]
    if w.dtype == BF16:
        out = jnp.dot(y.astype(BF16), w, preferred_element_type=F32)
    else:
        out = jnp.dot(y, w, precision=HIGHEST, preferred_element_type=F32)
    if out_scale != 1.0:
        out = out * out_scale
    o_ref[...] = out.astype(o_ref.dtype)


def _norm_matmul(h, g, w, seq, *, scale=None, shift=None, out_dtype=F32, out_scale=1.0):
    t, d = h.shape
    n = w.shape[1]
    tm = _tile(seq, 256)
    tn = _tile(n, 2048)
    per_batch = seq // tm
    modulate = scale is not None
    in_specs = [pl.BlockSpec((tm, d), lambda j, i: (i, 0)),
                pl.BlockSpec((1, d), lambda j, i: (0, 0))]
    args = [h, g.reshape(1, d)]
    if modulate:
        mod_spec = pl.BlockSpec((None, 1, d), lambda j, i: (i // per_batch, 0, 0))
        in_specs += [mod_spec, mod_spec]
        args += [scale, shift]
    in_specs.append(pl.BlockSpec((d, tn), lambda j, i: (0, j)))
    args.append(w)
    return pl.pallas_call(
        functools.partial(_norm_matmul_kernel, modulate=modulate, out_scale=out_scale),
        grid=(n // tn, t // tm),
        in_specs=in_specs,
        out_specs=pl.BlockSpec((tm, tn), lambda j, i: (i, j)),
        out_shape=jax.ShapeDtypeStruct((t, n), out_dtype),
        compiler_params=_params(("arbitrary", "arbitrary"), 48),
        name="norm_matmul",
    )(*args)


def _rope_table_kernel(pos_ref, freq_ref, c_ref, s1_ref, s2_ref):
    half = ROPE_DIM // 2
    ang = pos_ref[...].astype(F32) * freq_ref[...]
    lane = lax.broadcasted_iota(jnp.int32, ang.shape, 1)
    cos = jnp.cos(ang)
    sin = jnp.sin(ang)
    c_ref[...] = jnp.where(lane < 2 * half, cos, 0.0)
    s1_ref[...] = jnp.where(lane < half, -sin, 0.0)
    s2_ref[...] = jnp.where((lane >= half) & (lane < 2 * half), sin, 0.0)


def _rope_tables(positions):
    t = positions.size
    half = ROPE_DIM // 2
    inv_freq = ROPE_THETA ** (-jnp.arange(0, ROPE_DIM, 2, dtype=F32) / ROPE_DIM)
    freq = jnp.tile(inv_freq, V7X_LANES // half).reshape(1, V7X_LANES)
    tm = _tile(t, 512)
    spec = pl.BlockSpec((tm, V7X_LANES), lambda i: (i, 0))
    shape = jax.ShapeDtypeStruct((t, V7X_LANES), F32)
    return pl.pallas_call(
        _rope_table_kernel,
        grid=(t // tm,),
        in_specs=[pl.BlockSpec((tm, 1), lambda i: (i, 0)),
                  pl.BlockSpec((1, V7X_LANES), lambda i: (0, 0))],
        out_specs=[spec, spec, spec],
        out_shape=[shape, shape, shape],
        compiler_params=_params(("arbitrary",), 16),
        name="rope_tables",
    )(positions.reshape(t, 1), freq)


def _rope_chunk(z, c, s1, s2):
    half = ROPE_DIM // 2
    return (z * c + pltpu.roll(z, V7X_LANES - half, axis=1) * s1
            + pltpu.roll(z, half, axis=1) * s2)


def _mla_q_kernel(x_ref, g_ref, w_ref, c_ref, s1_ref, s2_ref, o_ref, *, heads, scale):
    y = _rms(x_ref[...], g_ref[...]).astype(BF16)
    q = jnp.dot(y, w_ref[...], preferred_element_type=F32) * scale
    c, s1, s2 = c_ref[...], s1_ref[...], s2_ref[...]
    for h in range(heads):
        base = h * 2 * HEAD_DIM
        o_ref[:, base:base + HEAD_DIM] = q[:, base:base + HEAD_DIM].astype(BF16)
        z = q[:, base + HEAD_DIM:base + 2 * HEAD_DIM]
        o_ref[:, base + HEAD_DIM:base + 2 * HEAD_DIM] = _rope_chunk(z, c, s1, s2).astype(BF16)


def _mla_kv_kernel(x_ref, r_ref, g_ref, w_ref, c_ref, s1_ref, s2_ref, k_ref, v_ref, *, heads):
    y = _rms(x_ref[...], g_ref[...]).astype(BF16)
    kv = jnp.dot(y, w_ref[...], preferred_element_type=F32)
    kr = _rope_chunk(r_ref[...], c_ref[...], s1_ref[...], s2_ref[...]).astype(BF16)
    for h in range(heads):
        base = h * 2 * HEAD_DIM
        k_ref[:, base:base + HEAD_DIM] = kv[:, h * HEAD_DIM:(h + 1) * HEAD_DIM].astype(BF16)
        k_ref[:, base + HEAD_DIM:base + 2 * HEAD_DIM] = kr
    v_ref[...] = kv[:, heads * HEAD_DIM:].astype(BF16)


def _mla_qkv(proj, q_norm_g, kv_norm_g, wq, wkv, tables, heads, q_lora, kv_lora, scale):
    t = proj.shape[0]
    tm = _tile(t, 256)
    dq = heads * 2 * HEAD_DIM
    assert q_lora == kv_lora and q_lora % V7X_LANES == 0
    tab_spec = pl.BlockSpec((tm, V7X_LANES), lambda i: (i, 0))
    rope_block = (q_lora + kv_lora) // V7X_LANES
    q = pl.pallas_call(
        functools.partial(_mla_q_kernel, heads=heads, scale=scale),
        grid=(t // tm,),
        in_specs=[pl.BlockSpec((tm, q_lora), lambda i: (i, 0)),
                  pl.BlockSpec((1, q_lora), lambda i: (0, 0)),
                  pl.BlockSpec((q_lora, dq), lambda i: (0, 0)),
                  tab_spec, tab_spec, tab_spec],
        out_specs=pl.BlockSpec((tm, dq), lambda i: (i, 0)),
        out_shape=jax.ShapeDtypeStruct((t, dq), BF16),
        compiler_params=_params(("arbitrary",), 40),
        name="mla_q",
    )(proj, q_norm_g.reshape(1, q_lora), wq, *tables)
    k, v = pl.pallas_call(
        functools.partial(_mla_kv_kernel, heads=heads),
        grid=(t // tm,),
        in_specs=[pl.BlockSpec((tm, kv_lora), lambda i: (i, 1)),
                  pl.BlockSpec((tm, V7X_LANES), lambda i: (i, rope_block)),
                  pl.BlockSpec((1, kv_lora), lambda i: (0, 0)),
                  pl.BlockSpec((kv_lora, dq), lambda i: (0, 0)),
                  tab_spec, tab_spec, tab_spec],
        out_specs=[pl.BlockSpec((tm, dq), lambda i: (i, 0)),
                   pl.BlockSpec((tm, heads * HEAD_DIM), lambda i: (i, 0))],
        out_shape=[jax.ShapeDtypeStruct((t, dq), BF16),
                   jax.ShapeDtypeStruct((t, heads * HEAD_DIM), BF16)],
        compiler_params=_params(("arbitrary",), 40),
        name="mla_kv",
    )(proj, proj, kv_norm_g.reshape(1, kv_lora), wkv, *tables)
    return q, k, v


FLASH_HEADS_PER_STEP = 4
LOG2E = 1.4426950408889634


def _flash_kernel(*refs, tq, dk, dv, hp, decay):
    if decay:
        q_ref, k_ref, v_ref, fq_ref, fk_ref, o_ref = refs
    else:
        q_ref, k_ref, v_ref, o_ref = refs
    qi = pl.program_id(2)
    qs = [q_ref[:, a * dk:(a + 1) * dk] for a in range(hp)]
    if decay:
        fq_all = fq_ref[...]
        lane = lax.broadcasted_iota(jnp.int32, fq_all.shape, 1)
        fqs = [jnp.sum(jnp.where(lane == pl.program_id(1) * hp + a, fq_all, 0.0), axis=-1,
                       keepdims=True) for a in range(hp)]

    def block(j, carry, masked):
        start = pl.multiple_of(j * tq, tq)
        out = []
        for a in range(hp):
            m, l, acc = carry[a]
            k = k_ref[pl.ds(start, tq), a * dk:(a + 1) * dk]
            v = v_ref[pl.ds(start, tq), a * dv:(a + 1) * dv]
            s = lax.dot_general(qs[a], k, (((1,), (1,)), ((), ())),
                                preferred_element_type=F32)
            if decay:
                s = s + (fqs[a] - fk_ref[a, :, pl.ds(start, tq)])
            if masked:
                row = lax.broadcasted_iota(jnp.int32, s.shape, 0)
                col = lax.broadcasted_iota(jnp.int32, s.shape, 1)
                s = jnp.where(col <= row, s, -jnp.inf)
            m_new = jnp.maximum(m, jnp.max(s, axis=-1, keepdims=True))
            p = jnp.exp2(s - m_new)
            alpha = jnp.exp2(m - m_new)
            l = alpha * l + jnp.sum(p, axis=-1, keepdims=True)
            acc = alpha * acc + jnp.dot(p.astype(BF16), v, preferred_element_type=F32)
            out.append((m_new, l, acc))
        return tuple(out)

    init = tuple((jnp.full((tq, 1), -jnp.inf, F32), jnp.zeros((tq, 1), F32),
                  jnp.zeros((tq, dv), F32)) for _ in range(hp))
    carry = lax.fori_loop(0, qi, lambda j, c: block(j, c, False), init)
    final = block(qi, carry, True)
    for a in range(hp):
        _, l, acc = final[a]
        o_ref[:, a * dv:(a + 1) * dv] = (acc / l).astype(o_ref.dtype)


def _flash_attention(q, k, v, heads, dv, v_block0=0, fq=None, fk=None):
    b, s, _ = q.shape
    dk = q.shape[-1] // heads
    tq = _tile(s, 512)
    hp = min(FLASH_HEADS_PER_STEP, heads)
    assert heads % hp == 0 and v_block0 % hp == 0
    vb0 = v_block0 // hp
    decay = fq is not None
    in_specs = [pl.BlockSpec((None, tq, hp * dk), lambda bi, h, i: (bi, i, h)),
                pl.BlockSpec((None, s, hp * dk), lambda bi, h, i: (bi, 0, h)),
                pl.BlockSpec((None, s, hp * dv), lambda bi, h, i: (bi, 0, vb0 + h))]
    args = [q, k, v]
    if decay:
        in_specs += [pl.BlockSpec((None, tq, V7X_LANES), lambda bi, h, i: (bi, i, 0)),
                     pl.BlockSpec((None, hp, 1, s), lambda bi, h, i: (bi, h, 0, 0))]
        args += [fq, fk]
    return pl.pallas_call(
        functools.partial(_flash_kernel, tq=tq, dk=dk, dv=dv, hp=hp, decay=decay),
        grid=(b, heads // hp, s // tq),
        in_specs=in_specs,
        out_specs=pl.BlockSpec((None, tq, hp * dv), lambda bi, h, i: (bi, i, h)),
        out_shape=jax.ShapeDtypeStruct((b, s, heads * dv), BF16),
        compiler_params=_params(("arbitrary", "arbitrary", "arbitrary"), 40),
        name="flash_attention",
    )(*args)


def _out_proj_kernel(o_ref, w_ref, h_ref, g_ref, out_ref):
    a = jnp.dot(o_ref[...], w_ref[...], preferred_element_type=F32)
    out_ref[...] = h_ref[...] + g_ref[...] * a


def _out_proj_residual(o, w, h, gate, seq):
    t, dk = o.shape
    d = w.shape[1]
    tm = _tile(seq, 256)
    per_batch = seq // tm
    return pl.pallas_call(
        _out_proj_kernel,
        grid=(t // tm,),
        in_specs=[pl.BlockSpec((tm, dk), lambda i: (i, 0)),
                  pl.BlockSpec((dk, d), lambda i: (0, 0)),
                  pl.BlockSpec((tm, d), lambda i: (i, 0)),
                  pl.BlockSpec((None, 1, d), lambda i: (i // per_batch, 0, 0))],
        out_specs=pl.BlockSpec((tm, d), lambda i: (i, 0)),
        out_shape=jax.ShapeDtypeStruct((t, d), F32),
        compiler_params=_params(("arbitrary",), 40),
        name="out_proj_residual",
    )(o, w, h, gate)


def _forget_cumsum_kernel(f_ref, b_ref, o_ref, carry_ref, *, out_scale):
    @pl.when(pl.program_id(1) == 0)
    def _():
        carry_ref[...] = jnp.zeros_like(carry_ref)

    x = f_ref[...] + b_ref[...]
    log_f = jnp.minimum(x, 0.0) - jnp.log1p(jnp.exp(-jnp.abs(x)))
    ts = x.shape[0]
    row = lax.broadcasted_iota(jnp.int32, (ts, ts), 0)
    col = lax.broadcasted_iota(jnp.int32, (ts, ts), 1)
    tri = jnp.where(col <= row, 1.0, 0.0).astype(F32)
    c = jnp.dot(tri, log_f, precision=HIGHEST, preferred_element_type=F32) + carry_ref[...]
    o_ref[...] = c * out_scale
    carry_ref[...] = c[ts - 1:ts, :]


def _forget_cumsum(f_logit, b_pad, out_scale):
    b, s, lanes = f_logit.shape
    ts = _tile(s, 512)
    return pl.pallas_call(
        functools.partial(_forget_cumsum_kernel, out_scale=out_scale),
        grid=(b, s // ts),
        in_specs=[pl.BlockSpec((None, ts, lanes), lambda bi, i: (bi, i, 0)),
                  pl.BlockSpec((1, lanes), lambda bi, i: (0, 0))],
        out_specs=pl.BlockSpec((None, ts, lanes), lambda bi, i: (bi, i, 0)),
        out_shape=jax.ShapeDtypeStruct((b, s, lanes), F32),
        scratch_shapes=[pltpu.VMEM((1, lanes), F32)],
        compiler_params=_params(("arbitrary", "arbitrary"), 16),
        name="forget_cumsum",
    )(f_logit, b_pad)


def _moe_route_kernel(h_ref, g_ref, sc_ref, sh_ref, rw_ref, rb_ref, f_ref, idx_ref, w_ref):
    y = _rms(h_ref[...], g_ref[...]) * (1.0 + sc_ref[...]) + sh_ref[...]
    _store_row_contiguous(f_ref, y)
    logits = jnp.dot(y, rw_ref[...], precision=HIGHEST,
                     preferred_element_type=F32) + rb_ref[...]
    lane = lax.broadcasted_iota(jnp.int32, logits.shape, 1).astype(F32)
    vals, idxs = [], []
    for _ in range(TOP_K):
        m = jnp.max(logits, axis=-1, keepdims=True)
        ix = jnp.min(jnp.where(logits == m, lane, float(logits.shape[1])), axis=-1,
                     keepdims=True)
        vals.append(m)
        idxs.append(ix)
        logits = jnp.where(lane == ix, -jnp.inf, logits)
    exps = [jnp.exp(v - vals[0]) for v in vals]
    den = exps[0]
    for e in exps[1:]:
        den = den + e
    slot = lax.broadcasted_iota(jnp.int32, idx_ref.shape, 1)
    idx_out = jnp.zeros(idx_ref.shape, jnp.int32)
    w_out = jnp.zeros(w_ref.shape, F32)
    for k in range(TOP_K):
        idx_out = jnp.where(slot == k, idxs[k].astype(jnp.int32), idx_out)
        w_out = jnp.where(slot == k, exps[k] / den, w_out)
    idx_ref[...] = idx_out
    w_ref[...] = w_out


def _moe_route(h, g, scale, shift, rw_pad, rb_pad, seq):
    t, d = h.shape
    tm = _tile(seq, 256)
    per_batch = seq // tm
    lanes = rw_pad.shape[1]
    lines = d // V7X_LANES
    mod_spec = pl.BlockSpec((None, 1, d), lambda i: (i // per_batch, 0, 0))
    return pl.pallas_call(
        _moe_route_kernel,
        grid=(t // tm,),
        in_specs=[pl.BlockSpec((tm, d), lambda i: (i, 0)),
                  pl.BlockSpec((1, d), lambda i: (0, 0)),
                  mod_spec, mod_spec,
                  pl.BlockSpec((d, lanes), lambda i: (0, 0)),
                  pl.BlockSpec((1, lanes), lambda i: (0, 0))],
        out_specs=[pl.BlockSpec((tm * lines, V7X_LANES), lambda i: (i, 0)),
                   pl.BlockSpec((tm, TOP_K), lambda i: (i, 0)),
                   pl.BlockSpec((tm, TOP_K), lambda i: (i, 0))],
        out_shape=[jax.ShapeDtypeStruct((t * lines, V7X_LANES), F32),
                   jax.ShapeDtypeStruct((t, TOP_K), jnp.int32),
                   jax.ShapeDtypeStruct((t, TOP_K), F32)],
        compiler_params=_params(("arbitrary",), 32),
        name="moe_route",
    )(h, g.reshape(1, d), scale, shift, rw_pad, rb_pad)


def _routing_tables(top_idx, n_experts, tm):
    t, k = top_idx.shape
    n_tiles = (t * k) // tm + n_experts
    e_flat = top_idx.reshape(-1)
    onehot = (e_flat[:, None] == jnp.arange(n_experts, dtype=jnp.int32)[None, :]).astype(jnp.int32)
    incl = jnp.cumsum(onehot, axis=0)
    counts = incl[-1]
    rank = jnp.sum((incl - onehot) * onehot, axis=1)
    tiles_per = (counts + tm - 1) // tm
    tile_end = jnp.cumsum(tiles_per)
    tile_start = jnp.concatenate([jnp.zeros((1,), jnp.int32), tile_end]).astype(jnp.int32)
    row_start = tile_start[:-1] * tm
    pos = row_start[e_flat] + rank
    n_used = tile_end[-1]
    order = jnp.argsort(e_flat, stable=True).astype(jnp.int32)
    count_start = jnp.cumsum(counts) - counts
    rows = jnp.arange(n_tiles * tm, dtype=jnp.int32)
    row_expert = jnp.sum((rows[:, None] // tm >= tile_end[None, :]).astype(jnp.int32), axis=1)
    row_expert = jnp.minimum(row_expert, n_experts - 1)
    r = rows - row_start[row_expert]
    valid = r < counts[row_expert]
    src_flat = order[jnp.clip(count_start[row_expert] + r, 0, t * k - 1)]
    src_tok = jnp.where(valid, src_flat // k, 0)
    return (src_tok.astype(jnp.int32), pos.reshape(t, k), tile_start,
            n_used.reshape(1).astype(jnp.int32))


ROW_DMA_UNROLL = 8


def _row_copy(src_hbm, row, buf, slot, r, sem):
    lines = src_hbm.shape[1]
    dst = buf.at[slot, pl.ds(pl.multiple_of(r * lines, lines), lines), :]
    return pltpu.make_async_copy(src_hbm.at[row], dst, sem.at[slot])


def _start_rows(src_hbm, idx_ref, n, buf, slot, sem):
    def body(r, c):
        _row_copy(src_hbm, idx_ref[0, 0, r], buf, slot, r, sem).start()
        return c
    lax.fori_loop(0, n, body, 0, unroll=ROW_DMA_UNROLL)


def _wait_rows(src_hbm, n, buf, slot, sem):
    def body(r, c):
        _row_copy(src_hbm, 0, buf, slot, r, sem).wait()
        return c
    lax.fori_loop(0, n, body, 0, unroll=ROW_DMA_UNROLL)


def _dispatch_kernel(nt_ref, cur_ref, nxt_ref, f_hbm, o_ref, buf, sem, *, tm):
    i = pl.program_id(0)
    nt = nt_ref[0]

    @pl.when(i == 0)
    def _():
        _start_rows(f_hbm, cur_ref, tm, buf, 0, sem)

    @pl.when(i + 1 < nt)
    def _():
        _start_rows(f_hbm, nxt_ref, tm, buf, (i + 1) % 2, sem)

    @pl.when(i < nt)
    def _():
        slot = i % 2
        _wait_rows(f_hbm, tm, buf, slot, sem)
        x = _load_row_contiguous(buf.at[slot], 0, tm, f_hbm.shape[1])
        o_ref[...] = x.astype(o_ref.dtype)

    @pl.when(i >= nt)
    def _():
        o_ref[...] = jnp.zeros_like(o_ref)


def _moe_dispatch(f_rows, src_tok, n_used, tm):
    t, lines, lanes = f_rows.shape
    d = lines * lanes
    n_tiles = src_tok.shape[0] // tm
    src3 = src_tok.reshape(n_tiles, 1, tm)
    smem = functools.partial(pl.BlockSpec, memory_space=pltpu.SMEM)
    grid_spec = pltpu.PrefetchScalarGridSpec(
        num_scalar_prefetch=1,
        grid=(n_tiles,),
        in_specs=[smem((1, 1, tm), lambda i, nt: (i, 0, 0)),
                  smem((1, 1, tm), lambda i, nt: (jnp.minimum(i + 1, n_tiles - 1), 0, 0)),
                  pl.BlockSpec(memory_space=pl.ANY)],
        out_specs=pl.BlockSpec((tm, d), lambda i, nt: (i, 0)),
        scratch_shapes=[pltpu.VMEM((2, tm * lines, lanes), F32),
                        pltpu.SemaphoreType.DMA((2,))],
    )
    return pl.pallas_call(
        functools.partial(_dispatch_kernel, tm=tm),
        grid_spec=grid_spec,
        out_shape=jax.ShapeDtypeStruct((n_tiles * tm, d), BF16),
        compiler_params=_params(("arbitrary",), 32),
        name="moe_dispatch",
    )(n_used, src3, src3, f_rows)


def _run_expert_tiles(t0, t1, in_copies, out_copy, prologue, step):
    @pl.when(t1 > t0)
    def _():
        for cp in in_copies(t0, 0):
            cp.start()
        prologue()

        def body(t, c):
            slot = (t - t0) % 2
            for cp in in_copies(t, slot):
                cp.wait()

            @pl.when(t + 1 < t1)
            def _():
                for cp in in_copies(t + 1, 1 - slot):
                    cp.start()

            @pl.when(t - t0 >= 2)
            def _():
                out_copy(t - 2, slot).wait()

            step(slot)
            out_copy(t, slot).start()
            return c
        lax.fori_loop(t0, t1, body, 0)

        @pl.when(t1 - t0 >= 2)
        def _():
            out_copy(t1 - 2, (t1 - 2 - t0) % 2).wait()
        out_copy(t1 - 1, (t1 - 1 - t0) % 2).wait()


def _zero_tail_tiles(first, n_tiles, obuf, out_copy):
    obuf[0] = jnp.zeros(obuf.shape[1:], obuf.dtype)

    def body(t, c):
        out_copy(t, 0).start()
        out_copy(t, 0).wait()
        return c
    lax.fori_loop(first, n_tiles, body, 0)


def _moe_up_kernel(ts_ref, x_hbm, w_ref, bg_ref, bl_ref, o_hbm, wbf_ref, xbuf, obuf, xsem,
                   osem, *, tm, n_tiles):
    c = pl.program_id(0)
    e = pl.program_id(1)
    half = wbf_ref.shape[1] // 2
    group = V7X_MXU_DIM

    def in_copies(t, slot):
        rows = pl.ds(pl.multiple_of(t * tm, tm), tm)
        return [pltpu.make_async_copy(x_hbm.at[rows, :], xbuf.at[slot], xsem.at[slot])]

    def out_copy(t, slot):
        rows = pl.ds(pl.multiple_of(t * tm, tm), tm)
        return pltpu.make_async_copy(obuf.at[slot], o_hbm.at[c, rows, :], osem.at[slot])

    def convert():
        src = lax.broadcasted_iota(jnp.int32, (group, group), 0)
        dst = lax.broadcasted_iota(jnp.int32, (group, group), 1)
        want = jnp.where(dst < group // 2, 2 * dst, 2 * (dst - group // 2) + 1)
        perm = jnp.where(src == want, 1.0, 0.0).astype(BF16)
        for gi in range(wbf_ref.shape[1] // group):
            blk = w_ref[:, gi * group:(gi + 1) * group].astype(BF16)
            split = jnp.dot(blk, perm, preferred_element_type=F32).astype(BF16)
            lo = gi * (group // 2)
            wbf_ref[:, lo:lo + group // 2] = split[:, :group // 2]
            wbf_ref[:, half + lo:half + lo + group // 2] = split[:, group // 2:]

    def step(slot):
        u = jnp.dot(xbuf[slot], wbf_ref[...], preferred_element_type=F32)
        glu = jnp.minimum(u[:, :half] + bg_ref[...], SWIGLU_LIMIT)
        lin = jnp.clip(u[:, half:] + bl_ref[...], -SWIGLU_LIMIT, SWIGLU_LIMIT)
        act = glu * jax.nn.sigmoid(SWIGLU_ALPHA * glu) * (lin + 1.0)
        obuf[slot] = act.astype(obuf.dtype)

    _run_expert_tiles(ts_ref[e], ts_ref[e + 1], in_copies, out_copy, convert, step)

    @pl.when(e == pl.num_programs(1) - 1)
    def _():
        _zero_tail_tiles(ts_ref[e + 1], n_tiles, obuf, out_copy)


def _moe_down_kernel(ts_ref, a_hbm, w_ref, b_ref, o_hbm, wbf_ref, abuf, ybuf, asem, ysem, *,
                     tm, n_tiles, lines):
    e = pl.program_id(0)
    tf = a_hbm.shape[2]

    def in_copies(t, slot):
        rows = pl.ds(pl.multiple_of(t * tm, tm), tm)
        return [pltpu.make_async_copy(a_hbm.at[cc, rows, :],
                                      abuf.at[slot, :, pl.ds(cc * tf, tf)], asem.at[slot])
                for cc in range(a_hbm.shape[0])]

    def out_copy(t, slot):
        rows = pl.ds(pl.multiple_of(t * tm * lines, tm * lines), tm * lines)
        return pltpu.make_async_copy(ybuf.at[slot], o_hbm.at[rows, :], ysem.at[slot])

    def convert():
        wbf_ref[...] = w_ref[...].astype(BF16)

    def step(slot):
        y = jnp.dot(abuf[slot], wbf_ref[...], preferred_element_type=F32) + b_ref[...]
        _store_row_contiguous(ybuf.at[slot], y)

    _run_expert_tiles(ts_ref[e], ts_ref[e + 1], in_copies, out_copy, convert, step)

    @pl.when(e == pl.num_programs(0) - 1)
    def _():
        _zero_tail_tiles(ts_ref[e + 1], n_tiles, ybuf, out_copy)


def _moe_experts(x_sorted, tile_start, w1, b1g, b1l, w2, b2, layer, tm):
    p, d = x_sorted.shape
    n_experts, f = w2.shape[1], w2.shape[2]
    n_tiles = p // tm
    tf = _tile(f, 1024)
    n_chunks = f // tf
    lines = d // V7X_LANES
    hbm = pl.BlockSpec(memory_space=pl.ANY)

    up_spec = pltpu.PrefetchScalarGridSpec(
        num_scalar_prefetch=1,
        grid=(n_chunks, n_experts),
        in_specs=[hbm,
                  pl.BlockSpec((None, None, d, 2 * tf), lambda c, e, ts: (layer, e, 0, c)),
                  pl.BlockSpec((None, None, 1, tf), lambda c, e, ts: (layer, e, 0, c)),
                  pl.BlockSpec((None, None, 1, tf), lambda c, e, ts: (layer, e, 0, c))],
        out_specs=hbm,
        scratch_shapes=[pltpu.VMEM((d, 2 * tf), BF16),
                        pltpu.VMEM((2, tm, d), BF16), pltpu.VMEM((2, tm, tf), BF16),
                        pltpu.SemaphoreType.DMA((2,)), pltpu.SemaphoreType.DMA((2,))],
    )
    act = pl.pallas_call(
        functools.partial(_moe_up_kernel, tm=tm, n_tiles=n_tiles),
        grid_spec=up_spec,
        out_shape=jax.ShapeDtypeStruct((n_chunks, p, tf), BF16),
        compiler_params=_params(("arbitrary", "arbitrary"), 56),
        name="moe_up",
    )(tile_start, x_sorted, w1, b1g, b1l)

    down_spec = pltpu.PrefetchScalarGridSpec(
        num_scalar_prefetch=1,
        grid=(n_experts,),
        in_specs=[hbm,
                  pl.BlockSpec((None, None, f, d), lambda e, ts: (layer, e, 0, 0)),
                  pl.BlockSpec((None, None, 1, d), lambda e, ts: (layer, e, 0, 0))],
        out_specs=hbm,
        scratch_shapes=[pltpu.VMEM((f, d), BF16),
                        pltpu.VMEM((2, tm, f), BF16), pltpu.VMEM((2, tm * lines, V7X_LANES), F32),
                        pltpu.SemaphoreType.DMA((2,)), pltpu.SemaphoreType.DMA((2,))],
    )
    return pl.pallas_call(
        functools.partial(_moe_down_kernel, tm=tm, n_tiles=n_tiles, lines=lines),
        grid_spec=down_spec,
        out_shape=jax.ShapeDtypeStruct((p * lines, V7X_LANES), F32),
        compiler_params=_params(("arbitrary",), 56),
        name="moe_down",
    )(tile_start, act, w2, b2)


def _combine_kernel(*refs, tc, final_norm):
    if final_norm:
        cur_ref, nxt_ref, y_hbm, h_ref, w_ref, g_ref, fg_ref, o_ref, buf, mix_ref, sem = refs
    else:
        cur_ref, nxt_ref, y_hbm, h_ref, w_ref, g_ref, o_ref, buf, mix_ref, sem = refs
    i = pl.program_id(0)
    n = pl.num_programs(0)
    rows = TOP_K * tc
    lines = y_hbm.shape[1]

    @pl.when(i == 0)
    def _():
        _start_rows(y_hbm, cur_ref, rows, buf, 0, sem)

    @pl.when(i + 1 < n)
    def _():
        _start_rows(y_hbm, nxt_ref, rows, buf, (i + 1) % 2, sem)

    slot = i % 2
    _wait_rows(y_hbm, rows, buf, slot, sem)

    w = w_ref[...]
    blk = tc * lines
    mix = w[:, 0:1] * buf[slot, pl.ds(0, blk), :]
    for k in range(1, TOP_K):
        mix = mix + w[:, k:k + 1] * buf[slot, pl.ds(k * blk, blk), :]
    mix_ref[...] = mix
    out = h_ref[...] + g_ref[...] * _load_row_contiguous(mix_ref, 0, tc, lines)
    if final_norm:
        out = _rms(out, fg_ref[...])
    o_ref[...] = out


def _moe_combine(y_rows, pos, top_w, h, gate, seq, final_g=None):
    t, d = h.shape
    lines = d // V7X_LANES
    tc = _tile(seq, 128)
    n = t // tc
    per_batch = seq // tc
    pos3 = pos.reshape(n, tc, TOP_K).transpose(0, 2, 1).reshape(n, 1, TOP_K * tc)
    w_lines = jnp.repeat(top_w, lines, axis=0)
    smem = functools.partial(pl.BlockSpec, memory_space=pltpu.SMEM)
    final_norm = final_g is not None
    in_specs = [smem((1, 1, TOP_K * tc), lambda i: (i, 0, 0)),
                smem((1, 1, TOP_K * tc), lambda i: (jnp.minimum(i + 1, n - 1), 0, 0)),
                pl.BlockSpec(memory_space=pl.ANY),
                pl.BlockSpec((tc, d), lambda i: (i, 0)),
                pl.BlockSpec((tc * lines, TOP_K), lambda i: (i, 0)),
                pl.BlockSpec((None, 1, d), lambda i: (i // per_batch, 0, 0))]
    args = [pos3, pos3, y_rows, h, w_lines, gate]
    if final_norm:
        in_specs.append(pl.BlockSpec((1, d), lambda i: (0, 0)))
        args.append(final_g.reshape(1, d))
    return pl.pallas_call(
        functools.partial(_combine_kernel, tc=tc, final_norm=final_norm),
        grid=(n,),
        in_specs=in_specs,
        out_specs=pl.BlockSpec((tc, d), lambda i: (i, 0)),
        out_shape=jax.ShapeDtypeStruct((t, d), F32),
        scratch_shapes=[pltpu.VMEM((2, TOP_K * tc * lines, V7X_LANES), F32),
                        pltpu.VMEM((tc * lines, V7X_LANES), F32),
                        pltpu.SemaphoreType.DMA((2,))],
        compiler_params=_params(("arbitrary",), 32),
        name="moe_combine",
    )(*args)


def _moe_layer(h, layer, seq, norm_g, scale, shift, gate, router_w, router_b,
               moe_w1, moe_b1, moe_w2, moe_b2, final_g=None):
    n_experts = router_w.shape[-1]
    d = h.shape[1]
    lanes = V7X_LANES * pl.cdiv(n_experts, V7X_LANES)
    rw_pad = jnp.pad(router_w[layer], ((0, 0), (0, lanes - n_experts)))
    rb_pad = jnp.pad(router_b[layer], (0, lanes - n_experts),
                     constant_values=NEG_BIG).reshape(1, lanes)
    f_in, top_idx, top_w = _moe_route(h, norm_g, scale, shift, rw_pad, rb_pad, seq)
    tm = 256
    src_tok, pos, tile_start, n_used = _routing_tables(top_idx, n_experts, tm)
    lines = d // V7X_LANES
    x_sorted = _moe_dispatch(f_in.reshape(-1, lines, V7X_LANES), src_tok, n_used, tm)
    f2 = moe_b1.shape[-1]
    b1 = moe_b1.reshape(moe_b1.shape[0], n_experts, f2 // 2, 2)
    b1g = b1[..., 0].reshape(moe_b1.shape[0], n_experts, 1, f2 // 2)
    b1l = b1[..., 1].reshape(moe_b1.shape[0], n_experts, 1, f2 // 2)
    b2 = moe_b2.reshape(moe_b2.shape[0], n_experts, 1, d)
    y_sorted = _moe_experts(x_sorted, tile_start, moe_w1, b1g, b1l, moe_w2, b2, layer, tm)
    return _moe_combine(y_sorted.reshape(-1, lines, V7X_LANES), pos, top_w, h, gate, seq,
                        final_g)


def _mla_weight_layouts(w_in, w_q_up, w_kv_up, heads, q_lora, kv_lora):
    half = ROPE_DIM // 2
    pad = HEAD_DIM - ROPE_DIM
    d = w_in.shape[0]
    rope = w_in[:, q_lora + kv_lora:]
    w_in_p = jnp.concatenate(
        [w_in[:, :q_lora + kv_lora], rope[:, 0::2], rope[:, 1::2], jnp.zeros((d, pad), F32)],
        axis=1).astype(BF16)
    wq = w_q_up.reshape(q_lora, heads, HEAD_DIM + ROPE_DIM)
    wq_rope = wq[:, :, HEAD_DIM:]
    wq_p = jnp.concatenate(
        [wq[:, :, :HEAD_DIM], wq_rope[:, :, 0::2], wq_rope[:, :, 1::2],
         jnp.zeros((q_lora, heads, pad), F32)], axis=2)
    wq_p = wq_p.reshape(q_lora, heads * 2 * HEAD_DIM).astype(BF16)
    wkv = w_kv_up.reshape(kv_lora, heads, 2 * HEAD_DIM)
    wkv_p = jnp.concatenate([wkv[:, :, :HEAD_DIM].reshape(kv_lora, heads * HEAD_DIM),
                             wkv[:, :, HEAD_DIM:].reshape(kv_lora, heads * HEAD_DIM)],
                            axis=1).astype(BF16)
    assert half * 2 == ROPE_DIM and w_in_p.shape[1] == q_lora + kv_lora + HEAD_DIM
    return w_in_p, wq_p, wkv_p


def kernel(x, c, positions, mod_w, mod_b, attn_norm_g, ffn_norm_g, mla_w_in, mla_q_norm_g,
           mla_kv_norm_g, mla_w_q_up, mla_w_kv_up, mla_w_out, shared_norm_g, shared_w_kvf,
           shared_b_f, fox_w_q, fox_w_out, router_w, router_b, moe_w1, moe_b1, moe_w2, moe_b2,
           final_norm_g):
    b, s, d = x.shape
    depth = mod_w.shape[0]
    n_a = mla_w_in.shape[0]
    heads = d // HEAD_DIM
    q_lora = mla_q_norm_g.shape[-1]
    kv_lora = mla_kv_norm_g.shape[-1]
    hd = heads * HEAD_DIM
    t = b * s

    c_pad = jnp.pad(c, ((0, (-b) % V7X_SUBLANES), (0, 0)))
    mod = _modulation(c_pad, mod_w, mod_b)[:, :b, :]
    tables = _rope_tables(positions)

    h = x.reshape(t, d)
    kv_sh = fq = fk = None
    for layer in range(depth):
        sh_a, sc_a, g_a, sh_f, sc_f, g_f = (
            mod[layer, :, i * d:(i + 1) * d].reshape(b, 1, d) for i in range(N_MOD))
        if layer < n_a:
            w_in_p, wq_p, wkv_p = _mla_weight_layouts(
                mla_w_in[layer], mla_w_q_up[layer], mla_w_kv_up[layer], heads, q_lora, kv_lora)
            proj = _norm_matmul(h, attn_norm_g[layer], w_in_p, s, scale=sc_a, shift=sh_a)
            q, k, v = _mla_qkv(proj, mla_q_norm_g[layer], mla_kv_norm_g[layer], wq_p, wkv_p,
                               tables, heads, q_lora, kv_lora,
                               LOG2E * (HEAD_DIM + ROPE_DIM) ** -0.5)
            o = _flash_attention(q.reshape(b, s, -1), k.reshape(b, s, -1),
                                 v.reshape(b, s, -1), heads, HEAD_DIM)
            w_out = mla_w_out[layer].astype(BF16)
        else:
            j = layer - n_a
            q = _norm_matmul(h, attn_norm_g[layer], fox_w_q[j].astype(BF16), s, scale=sc_a,
                             shift=sh_a, out_dtype=BF16,
                             out_scale=LOG2E * HEAD_DIM ** -0.5)
            o = _flash_attention(q.reshape(b, s, hd), kv_sh, kv_sh, heads, HEAD_DIM,
                                 v_block0=heads, fq=fq, fk=fk)
            w_out = fox_w_out[j].astype(BF16)
        h = _out_proj_residual(o.reshape(t, hd), w_out, h, g_a, s)
        last = layer == depth - 1
        h = _moe_layer(h, layer, s, ffn_norm_g[layer], sc_f, sh_f, g_f, router_w, router_b,
                       moe_w1, moe_b1, moe_w2, moe_b2, final_norm_g if last else None)
        if layer == n_a - 1:
            kv_sh = _norm_matmul(h, shared_norm_g, shared_w_kvf[:, :2 * hd].astype(BF16), s,
                                 out_dtype=BF16).reshape(b, s, 2 * hd)
            w_f = jnp.pad(shared_w_kvf[:, 2 * hd:], ((0, 0), (0, V7X_LANES - heads)))
            f_logit = _norm_matmul(h, shared_norm_g, w_f, s)
            b_pad = jnp.pad(shared_b_f, (0, V7X_LANES - heads)).reshape(1, V7X_LANES)
            fq = _forget_cumsum(f_logit.reshape(b, s, V7X_LANES), b_pad, LOG2E)
            fk = fq[:, :, :heads].transpose(0, 2, 1).reshape(b, heads, 1, s)
    return h.reshape(b, s, d)
```

```python
import functools

import jax
import jax.numpy as jnp
from jax import lax
from jax.experimental import pallas as pl
from jax.experimental.pallas import tpu as pltpu

F32 = jnp.float32
BF16 = jnp.bfloat16
HIGHEST = lax.Precision.HIGHEST

RMS_EPS = 1e-6
HEAD_DIM = 128
ROPE_DIM = 64
ROPE_THETA = 10000.0
TOP_K = 4
SWIGLU_ALPHA = 1.702
SWIGLU_LIMIT = 7.0
N_MOD = 6

V7X_LANES = 128
V7X_SUBLANES = 8
V7X_MXU_DIM = 256
V7X_VMEM_BYTES = 64 * 1024 * 1024
MIB = 1024 * 1024

NEG_BIG = -1e30


def _params(semantics, vmem_mib):
    assert vmem_mib * MIB < V7X_VMEM_BYTES
    return pltpu.CompilerParams(dimension_semantics=semantics,
                                vmem_limit_bytes=vmem_mib * MIB)


def _tile(n, pref):
    if n <= pref:
        return n
    t = pref
    while n % t:
        t //= 2
    return t


def _rms(x, g):
    ms = jnp.mean(x * x, axis=-1, keepdims=True)
    return x * lax.rsqrt(ms + RMS_EPS) * g


def _store_row_contiguous(ref, x):
    rows, d = x.shape
    lines = d // V7X_LANES
    for s in range(lines):
        ref[pl.ds(s, rows, stride=lines), :] = x[:, s * V7X_LANES:(s + 1) * V7X_LANES]


def _load_row_contiguous(ref, base, rows, lines):
    return jnp.concatenate(
        [ref[pl.ds(base + s, rows, stride=lines), :] for s in range(lines)], axis=1)


def _mod_kernel(c_ref, w_ref, b_ref, o_ref):
    o_ref[...] = jnp.dot(c_ref[...], w_ref[...], precision=HIGHEST,
                         preferred_element_type=F32) + b_ref[...]


def _modulation(c_pad, mod_w, mod_b):
    n_layers, d, n = mod_w.shape
    rows = c_pad.shape[0]
    tn = _tile(n, 1024)
    return pl.pallas_call(
        _mod_kernel,
        grid=(n_layers, n // tn),
        in_specs=[
            pl.BlockSpec((rows, d), lambda l, j: (0, 0)),
            pl.BlockSpec((None, d, tn), lambda l, j: (l, 0, j)),
            pl.BlockSpec((None, 1, tn), lambda l, j: (l, 0, j)),
        ],
        out_specs=pl.BlockSpec((None, rows, tn), lambda l, j: (l, 0, j)),
        out_shape=jax.ShapeDtypeStruct((n_layers, rows, n), F32),
        compiler_params=_params(("arbitrary", "arbitrary"), 40),
        name="modulation",
    )(c_pad, mod_w, mod_b.reshape(n_layers, 1, n))


def _norm_matmul_kernel(*refs, modulate, out_scale):
    if modulate:
        h_ref, g_ref, sc_ref, sh_ref, w_ref, o_ref = refs
    else:
        h_ref, g_ref, w_ref, o_ref = refs
    y = _rms(h_ref[...], g_ref[...])
    if modulate:
        y = y * (1.0 + sc_ref[...]) + sh_ref[...]
    w = w_ref[...]
    if w.dtype == BF16:
        out = jnp.dot(y.astype(BF16), w, preferred_element_type=F32)
    else:
        out = jnp.dot(y, w, precision=HIGHEST, preferred_element_type=F32)
    if out_scale != 1.0:
        out = out * out_scale
    o_ref[...] = out.astype(o_ref.dtype)


def _norm_matmul(h, g, w, seq, *, scale=None, shift=None, out_dtype=F32, out_scale=1.0):
    t, d = h.shape
    n = w.shape[1]
    tm = _tile(seq, 256)
    tn = _tile(n, 2048)
    per_batch = seq // tm
    modulate = scale is not None
    in_specs = [pl.BlockSpec((tm, d), lambda j, i: (i, 0)),
                pl.BlockSpec((1, d), lambda j, i: (0, 0))]
    args = [h, g.reshape(1, d)]
    if modulate:
        mod_spec = pl.BlockSpec((None, 1, d), lambda j, i: (i // per_batch, 0, 0))
        in_specs += [mod_spec, mod_spec]
        args += [scale, shift]
    in_specs.append(pl.BlockSpec((d, tn), lambda j, i: (0, j)))
    args.append(w)
    return pl.pallas_call(
        functools.partial(_norm_matmul_kernel, modulate=modulate, out_scale=out_scale),
        grid=(n // tn, t // tm),
        in_specs=in_specs,
        out_specs=pl.BlockSpec((tm, tn), lambda j, i: (i, j)),
        out_shape=jax.ShapeDtypeStruct((t, n), out_dtype),
        compiler_params=_params(("arbitrary", "arbitrary"), 48),
        name="norm_matmul",
    )(*args)


def _rope_table_kernel(pos_ref, freq_ref, c_ref, s1_ref, s2_ref):
    half = ROPE_DIM // 2
    ang = pos_ref[...].astype(F32) * freq_ref[...]
    lane = lax.broadcasted_iota(jnp.int32, ang.shape, 1)
    cos = jnp.cos(ang)
    sin = jnp.sin(ang)
    c_ref[...] = jnp.where(lane < 2 * half, cos, 0.0)
    s1_ref[...] = jnp.where(lane < half, -sin, 0.0)
    s2_ref[...] = jnp.where((lane >= half) & (lane < 2 * half), sin, 0.0)


def _rope_tables(positions):
    t = positions.size
    half = ROPE_DIM // 2
    inv_freq = ROPE_THETA ** (-jnp.arange(0, ROPE_DIM, 2, dtype=F32) / ROPE_DIM)
    freq = jnp.tile(inv_freq, V7X_LANES // half).reshape(1, V7X_LANES)
    tm = _tile(t, 512)
    spec = pl.BlockSpec((tm, V7X_LANES), lambda i: (i, 0))
    shape = jax.ShapeDtypeStruct((t, V7X_LANES), F32)
    return pl.pallas_call(
        _rope_table_kernel,
        grid=(t // tm,),
        in_specs=[pl.BlockSpec((tm, 1), lambda i: (i, 0)),
                  pl.BlockSpec((1, V7X_LANES), lambda i: (0, 0))],
        out_specs=[spec, spec, spec],
        out_shape=[shape, shape, shape],
        compiler_params=_params(("arbitrary",), 16),
        name="rope_tables",
    )(positions.reshape(t, 1), freq)


def _rope_chunk(z, c, s1, s2):
    half = ROPE_DIM // 2
    return (z * c + pltpu.roll(z, V7X_LANES - half, axis=1) * s1
            + pltpu.roll(z, half, axis=1) * s2)


def _mla_q_kernel(x_ref, g_ref, w_ref, c_ref, s1_ref, s2_ref, o_ref, *, heads, scale):
    y = _rms(x_ref[...], g_ref[...]).astype(BF16)
    q = jnp.dot(y, w_ref[...], preferred_element_type=F32) * scale
    c, s1, s2 = c_ref[...], s1_ref[...], s2_ref[...]
    for h in range(heads):
        base = h * 2 * HEAD_DIM
        o_ref[:, base:base + HEAD_DIM] = q[:, base:base + HEAD_DIM].astype(BF16)
        z = q[:, base + HEAD_DIM:base + 2 * HEAD_DIM]
        o_ref[:, base + HEAD_DIM:base + 2 * HEAD_DIM] = _rope_chunk(z, c, s1, s2).astype(BF16)


def _mla_kv_kernel(x_ref, r_ref, g_ref, w_ref, c_ref, s1_ref, s2_ref, k_ref, v_ref, *, heads):
    y = _rms(x_ref[...], g_ref[...]).astype(BF16)
    kv = jnp.dot(y, w_ref[...], preferred_element_type=F32)
    kr = _rope_chunk(r_ref[...], c_ref[...], s1_ref[...], s2_ref[...]).astype(BF16)
    for h in range(heads):
        base = h * 2 * HEAD_DIM
        k_ref[:, base:base + HEAD_DIM] = kv[:, h * HEAD_DIM:(h + 1) * HEAD_DIM].astype(BF16)
        k_ref[:, base + HEAD_DIM:base + 2 * HEAD_DIM] = kr
    v_ref[...] = kv[:, heads * HEAD_DIM:].astype(BF16)


def _mla_qkv(proj, q_norm_g, kv_norm_g, wq, wkv, tables, heads, q_lora, kv_lora, scale):
    t = proj.shape[0]
    tm = _tile(t, 256)
    dq = heads * 2 * HEAD_DIM
    assert q_lora == kv_lora and q_lora % V7X_LANES == 0
    tab_spec = pl.BlockSpec((tm, V7X_LANES), lambda i: (i, 0))
    rope_block = (q_lora + kv_lora) // V7X_LANES
    q = pl.pallas_call(
        functools.partial(_mla_q_kernel, heads=heads, scale=scale),
        grid=(t // tm,),
        in_specs=[pl.BlockSpec((tm, q_lora), lambda i: (i, 0)),
                  pl.BlockSpec((1, q_lora), lambda i: (0, 0)),
                  pl.BlockSpec((q_lora, dq), lambda i: (0, 0)),
                  tab_spec, tab_spec, tab_spec],
        out_specs=pl.BlockSpec((tm, dq), lambda i: (i, 0)),
        out_shape=jax.ShapeDtypeStruct((t, dq), BF16),
        compiler_params=_params(("arbitrary",), 40),
        name="mla_q",
    )(proj, q_norm_g.reshape(1, q_lora), wq, *tables)
    k, v = pl.pallas_call(
        functools.partial(_mla_kv_kernel, heads=heads),
        grid=(t // tm,),
        in_specs=[pl.BlockSpec((tm, kv_lora), lambda i: (i, 1)),
                  pl.BlockSpec((tm, V7X_LANES), lambda i: (i, rope_block)),
                  pl.BlockSpec((1, kv_lora), lambda i: (0, 0)),
                  pl.BlockSpec((kv_lora, dq), lambda i: (0, 0)),
                  tab_spec, tab_spec, tab_spec],
        out_specs=[pl.BlockSpec((tm, dq), lambda i: (i, 0)),
                   pl.BlockSpec((tm, heads * HEAD_DIM), lambda i: (i, 0))],
        out_shape=[jax.ShapeDtypeStruct((t, dq), BF16),
                   jax.ShapeDtypeStruct((t, heads * HEAD_DIM), BF16)],
        compiler_params=_params(("arbitrary",), 40),
        name="mla_kv",
    )(proj, proj, kv_norm_g.reshape(1, kv_lora), wkv, *tables)
    return q, k, v


FLASH_HEADS_PER_STEP = 4
LOG2E = 1.4426950408889634


def _flash_kernel(*refs, tq, dk, dv, hp, decay):
    if decay:
        q_ref, k_ref, v_ref, fq_ref, fk_ref, o_ref = refs
    else:
        q_ref, k_ref, v_ref, o_ref = refs
    qi = pl.program_id(2)
    qs = [q_ref[:, a * dk:(a + 1) * dk] for a in range(hp)]
    if decay:
        fq_all = fq_ref[...]
        lane = lax.broadcasted_iota(jnp.int32, fq_all.shape, 1)
        fqs = [jnp.sum(jnp.where(lane == pl.program_id(1) * hp + a, fq_all, 0.0), axis=-1,
                       keepdims=True) for a in range(hp)]

    def block(j, carry, masked):
        start = pl.multiple_of(j * tq, tq)
        out = []
        for a in range(hp):
            m, l, acc = carry[a]
            k = k_ref[pl.ds(start, tq), a * dk:(a + 1) * dk]
            v = v_ref[pl.ds(start, tq), a * dv:(a + 1) * dv]
            s = lax.dot_general(qs[a], k, (((1,), (1,)), ((), ())),
                                preferred_element_type=F32)
            if decay:
                s = s + (fqs[a] - fk_ref[a, :, pl.ds(start, tq)])
            if masked:
                row = lax.broadcasted_iota(jnp.int32, s.shape, 0)
                col = lax.broadcasted_iota(jnp.int32, s.shape, 1)
                s = jnp.where(col <= row, s, -jnp.inf)
            m_new = jnp.maximum(m, jnp.max(s, axis=-1, keepdims=True))
            p = jnp.exp2(s - m_new)
            alpha = jnp.exp2(m - m_new)
            l = alpha * l + jnp.sum(p, axis=-1, keepdims=True)
            acc = alpha * acc + jnp.dot(p.astype(BF16), v, preferred_element_type=F32)
            out.append((m_new, l, acc))
        return tuple(out)

    init = tuple((jnp.full((tq, 1), -jnp.inf, F32), jnp.zeros((tq, 1), F32),
                  jnp.zeros((tq, dv), F32)) for _ in range(hp))
    carry = lax.fori_loop(0, qi, lambda j, c: block(j, c, False), init)
    final = block(qi, carry, True)
    for a in range(hp):
        _, l, acc = final[a]
        o_ref[:, a * dv:(a + 1) * dv] = (acc / l).astype(o_ref.dtype)


def _flash_attention(q, k, v, heads, dv, v_block0=0, fq=None, fk=None):
    b, s, _ = q.shape
    dk = q.shape[-1] // heads
    tq = _tile(s, 512)
    hp = min(FLASH_HEADS_PER_STEP, heads)
    assert heads % hp == 0 and v_block0 % hp == 0
    vb0 = v_block0 // hp
    decay = fq is not None
    in_specs = [pl.BlockSpec((None, tq, hp * dk), lambda bi, h, i: (bi, i, h)),
                pl.BlockSpec((None, s, hp * dk), lambda bi, h, i: (bi, 0, h)),
                pl.BlockSpec((None, s, hp * dv), lambda bi, h, i: (bi, 0, vb0 + h))]
    args = [q, k, v]
    if decay:
        in_specs += [pl.BlockSpec((None, tq, V7X_LANES), lambda bi, h, i: (bi, i, 0)),
                     pl.BlockSpec((None, hp, 1, s), lambda bi, h, i: (bi, h, 0, 0))]
        args += [fq, fk]
    return pl.pallas_call(
        functools.partial(_flash_kernel, tq=tq, dk=dk, dv=dv, hp=hp, decay=decay),
        grid=(b, heads // hp, s // tq),
        in_specs=in_specs,
        out_specs=pl.BlockSpec((None, tq, hp * dv), lambda bi, h, i: (bi, i, h)),
        out_shape=jax.ShapeDtypeStruct((b, s, heads * dv), BF16),
        compiler_params=_params(("arbitrary", "arbitrary", "arbitrary"), 40),
        name="flash_attention",
    )(*args)


def _out_proj_kernel(o_ref, w_ref, h_ref, g_ref, out_ref):
    a = jnp.dot(o_ref[...], w_ref[...], preferred_element_type=F32)
    out_ref[...] = h_ref[...] + g_ref[...] * a


def _out_proj_residual(o, w, h, gate, seq):
    t, dk = o.shape
    d = w.shape[1]
    tm = _tile(seq, 256)
    per_batch = seq // tm
    return pl.pallas_call(
        _out_proj_kernel,
        grid=(t // tm,),
        in_specs=[pl.BlockSpec((tm, dk), lambda i: (i, 0)),
                  pl.BlockSpec((dk, d), lambda i: (0, 0)),
                  pl.BlockSpec((tm, d), lambda i: (i, 0)),
                  pl.BlockSpec((None, 1, d), lambda i: (i // per_batch, 0, 0))],
        out_specs=pl.BlockSpec((tm, d), lambda i: (i, 0)),
        out_shape=jax.ShapeDtypeStruct((t, d), F32),
        compiler_params=_params(("arbitrary",), 40),
        name="out_proj_residual",
    )(o, w, h, gate)


def _forget_cumsum_kernel(f_ref, b_ref, o_ref, carry_ref, *, out_scale):
    @pl.when(pl.program_id(1) == 0)
    def _():
        carry_ref[...] = jnp.zeros_like(carry_ref)

    x = f_ref[...] + b_ref[...]
    log_f = jnp.minimum(x, 0.0) - jnp.log1p(jnp.exp(-jnp.abs(x)))
    ts = x.shape[0]
    row = lax.broadcasted_iota(jnp.int32, (ts, ts), 0)
    col = lax.broadcasted_iota(jnp.int32, (ts, ts), 1)
    tri = jnp.where(col <= row, 1.0, 0.0).astype(F32)
    c = jnp.dot(tri, log_f, precision=HIGHEST, preferred_element_type=F32) + carry_ref[...]
    o_ref[...] = c * out_scale
    carry_ref[...] = c[ts - 1:ts, :]


def _forget_cumsum(f_logit, b_pad, out_scale):
    b, s, lanes = f_logit.shape
    ts = _tile(s, 512)
    return pl.pallas_call(
        functools.partial(_forget_cumsum_kernel, out_scale=out_scale),
        grid=(b, s // ts),
        in_specs=[pl.BlockSpec((None, ts, lanes), lambda bi, i: (bi, i, 0)),
                  pl.BlockSpec((1, lanes), lambda bi, i: (0, 0))],
        out_specs=pl.BlockSpec((None, ts, lanes), lambda bi, i: (bi, i, 0)),
        out_shape=jax.ShapeDtypeStruct((b, s, lanes), F32),
        scratch_shapes=[pltpu.VMEM((1, lanes), F32)],
        compiler_params=_params(("arbitrary", "arbitrary"), 16),
        name="forget_cumsum",
    )(f_logit, b_pad)


def _moe_route_kernel(h_ref, g_ref, sc_ref, sh_ref, rw_ref, rb_ref, f_ref, idx_ref, w_ref):
    y = _rms(h_ref[...], g_ref[...]) * (1.0 + sc_ref[...]) + sh_ref[...]
    _store_row_contiguous(f_ref, y)
    logits = jnp.dot(y, rw_ref[...], precision=HIGHEST,
                     preferred_element_type=F32) + rb_ref[...]
    lane = lax.broadcasted_iota(jnp.int32, logits.shape, 1).astype(F32)
    vals, idxs = [], []
    for _ in range(TOP_K):
        m = jnp.max(logits, axis=-1, keepdims=True)
        ix = jnp.min(jnp.where(logits == m, lane, float(logits.shape[1])), axis=-1,
                     keepdims=True)
        vals.append(m)
        idxs.append(ix)
        logits = jnp.where(lane == ix, -jnp.inf, logits)
    exps = [jnp.exp(v - vals[0]) for v in vals]
    den = exps[0]
    for e in exps[1:]:
        den = den + e
    slot = lax.broadcasted_iota(jnp.int32, idx_ref.shape, 1)
    idx_out = jnp.zeros(idx_ref.shape, jnp.int32)
    w_out = jnp.zeros(w_ref.shape, F32)
    for k in range(TOP_K):
        idx_out = jnp.where(slot == k, idxs[k].astype(jnp.int32), idx_out)
        w_out = jnp.where(slot == k, exps[k] / den, w_out)
    idx_ref[...] = idx_out
    w_ref[...] = w_out


def _moe_route(h, g, scale, shift, rw_pad, rb_pad, seq):
    t, d = h.shape
    tm = _tile(seq, 256)
    per_batch = seq // tm
    lanes = rw_pad.shape[1]
    lines = d // V7X_LANES
    mod_spec = pl.BlockSpec((None, 1, d), lambda i: (i // per_batch, 0, 0))
    return pl.pallas_call(
        _moe_route_kernel,
        grid=(t // tm,),
        in_specs=[pl.BlockSpec((tm, d), lambda i: (i, 0)),
                  pl.BlockSpec((1, d), lambda i: (0, 0)),
                  mod_spec, mod_spec,
                  pl.BlockSpec((d, lanes), lambda i: (0, 0)),
                  pl.BlockSpec((1, lanes), lambda i: (0, 0))],
        out_specs=[pl.BlockSpec((tm * lines, V7X_LANES), lambda i: (i, 0)),
                   pl.BlockSpec((tm, TOP_K), lambda i: (i, 0)),
                   pl.BlockSpec((tm, TOP_K), lambda i: (i, 0))],
        out_shape=[jax.ShapeDtypeStruct((t * lines, V7X_LANES), F32),
                   jax.ShapeDtypeStruct((t, TOP_K), jnp.int32),
                   jax.ShapeDtypeStruct((t, TOP_K), F32)],
        compiler_params=_params(("arbitrary",), 32),
        name="moe_route",
    )(h, g.reshape(1, d), scale, shift, rw_pad, rb_pad)


def _routing_tables(top_idx, n_experts, tm):
    t, k = top_idx.shape
    n_tiles = (t * k) // tm + n_experts
    e_flat = top_idx.reshape(-1)
    onehot = (e_flat[:, None] == jnp.arange(n_experts, dtype=jnp.int32)[None, :]).astype(jnp.int32)
    incl = jnp.cumsum(onehot, axis=0)
    counts = incl[-1]
    rank = jnp.sum((incl - onehot) * onehot, axis=1)
    tiles_per = (counts + tm - 1) // tm
    tile_end = jnp.cumsum(tiles_per)
    tile_start = jnp.concatenate([jnp.zeros((1,), jnp.int32), tile_end]).astype(jnp.int32)
    row_start = tile_start[:-1] * tm
    pos = row_start[e_flat] + rank
    n_used = tile_end[-1]
    order = jnp.argsort(e_flat, stable=True).astype(jnp.int32)
    count_start = jnp.cumsum(counts) - counts
    rows = jnp.arange(n_tiles * tm, dtype=jnp.int32)
    row_expert = jnp.sum((rows[:, None] // tm >= tile_end[None, :]).astype(jnp.int32), axis=1)
    row_expert = jnp.minimum(row_expert, n_experts - 1)
    r = rows - row_start[row_expert]
    valid = r < counts[row_expert]
    src_flat = order[jnp.clip(count_start[row_expert] + r, 0, t * k - 1)]
    src_tok = jnp.where(valid, src_flat // k, 0)
    return (src_tok.astype(jnp.int32), pos.reshape(t, k), tile_start,
            n_used.reshape(1).astype(jnp.int32))


ROW_DMA_UNROLL = 8


def _row_copy(src_hbm, row, buf, slot, r, sem):
    lines = src_hbm.shape[1]
    dst = buf.at[slot, pl.ds(pl.multiple_of(r * lines, lines), lines), :]
    return pltpu.make_async_copy(src_hbm.at[row], dst, sem.at[slot])


def _start_rows(src_hbm, idx_ref, n, buf, slot, sem):
    def body(i, c):
        for prio in range(2):
            r = 2 * i + prio
            _row_copy(src_hbm, idx_ref[0, 0, r], buf, slot, r, sem).start(priority=prio)
        return c
    assert n % 2 == 0
    lax.fori_loop(0, n // 2, body, 0, unroll=ROW_DMA_UNROLL // 2)


def _wait_rows(src_hbm, n, buf, slot, sem):
    def body(r, c):
        _row_copy(src_hbm, 0, buf, slot, r, sem).wait()
        return c
    lax.fori_loop(0, n, body, 0, unroll=ROW_DMA_UNROLL)


def _dispatch_kernel(nt_ref, cur_ref, nxt_ref, f_hbm, o_ref, buf, sem, *, tm):
    i = pl.program_id(0)
    nt = nt_ref[0]

    @pl.when(i == 0)
    def _():
        _start_rows(f_hbm, cur_ref, tm, buf, 0, sem)

    @pl.when(i + 1 < nt)
    def _():
        _start_rows(f_hbm, nxt_ref, tm, buf, (i + 1) % 2, sem)

    @pl.when(i < nt)
    def _():
        slot = i % 2
        _wait_rows(f_hbm, tm, buf, slot, sem)
        x = _load_row_contiguous(buf.at[slot], 0, tm, f_hbm.shape[1])
        o_ref[...] = x.astype(o_ref.dtype)

    @pl.when(i >= nt)
    def _():
        o_ref[...] = jnp.zeros_like(o_ref)


def _moe_dispatch(f_rows, src_tok, n_used, tm):
    t, lines, lanes = f_rows.shape
    d = lines * lanes
    n_tiles = src_tok.shape[0] // tm
    src3 = src_tok.reshape(n_tiles, 1, tm)
    smem = functools.partial(pl.BlockSpec, memory_space=pltpu.SMEM)
    grid_spec = pltpu.PrefetchScalarGridSpec(
        num_scalar_prefetch=1,
        grid=(n_tiles,),
        in_specs=[smem((1, 1, tm), lambda i, nt: (i, 0, 0)),
                  smem((1, 1, tm), lambda i, nt: (jnp.minimum(i + 1, n_tiles - 1), 0, 0)),
                  pl.BlockSpec(memory_space=pl.ANY)],
        out_specs=pl.BlockSpec((tm, d), lambda i, nt: (i, 0)),
        scratch_shapes=[pltpu.VMEM((2, tm * lines, lanes), F32),
                        pltpu.SemaphoreType.DMA((2,))],
    )
    return pl.pallas_call(
        functools.partial(_dispatch_kernel, tm=tm),
        grid_spec=grid_spec,
        out_shape=jax.ShapeDtypeStruct((n_tiles * tm, d), BF16),
        compiler_params=_params(("arbitrary",), 32),
        name="moe_dispatch",
    )(n_used, src3, src3, f_rows)


TILE_DMA_PRIORITY = 0


def _run_expert_tiles(t0, t1, in_copies, out_copy, prologue, step):
    @pl.when(t1 > t0)
    def _():
        for cp in in_copies(t0, 0):
            cp.start(priority=TILE_DMA_PRIORITY)
        prologue()

        def body(t, c):
            slot = (t - t0) % 2
            for cp in in_copies(t, slot):
                cp.wait()

            @pl.when(t + 1 < t1)
            def _():
                for cp in in_copies(t + 1, 1 - slot):
                    cp.start(priority=TILE_DMA_PRIORITY)

            @pl.when(t - t0 >= 2)
            def _():
                out_copy(t - 2, slot).wait()

            step(slot)
            out_copy(t, slot).start(priority=TILE_DMA_PRIORITY)
            return c
        lax.fori_loop(t0, t1, body, 0)

        @pl.when(t1 - t0 >= 2)
        def _():
            out_copy(t1 - 2, (t1 - 2 - t0) % 2).wait()
        out_copy(t1 - 1, (t1 - 1 - t0) % 2).wait()


def _zero_tail_tiles(first, n_tiles, obuf, out_copy):
    obuf[0] = jnp.zeros(obuf.shape[1:], obuf.dtype)

    def body(t, c):
        out_copy(t, 0).start()
        out_copy(t, 0).wait()
        return c
    lax.fori_loop(first, n_tiles, body, 0)


WEIGHT_DMA_PRIORITY = 1


def _stream_weights(step, n_steps, copy_for_step, wbuf):
    slot = step % 2

    @pl.when(step == 0)
    def _():
        copy_for_step(step, 0).start(priority=WEIGHT_DMA_PRIORITY)

    @pl.when(step + 1 < n_steps)
    def _():
        copy_for_step(step + 1, 1 - slot).start(priority=WEIGHT_DMA_PRIORITY)

    copy_for_step(step, slot).wait()
    return wbuf.at[slot]


def _moe_up_kernel(ts_ref, x_hbm, w_hbm, bg_ref, bl_ref, o_hbm, wbuf, wbf_ref, xbuf, obuf,
                   wsem, xsem, osem, *, tm, n_tiles, layer):
    c = pl.program_id(0)
    e = pl.program_id(1)
    n_experts = pl.num_programs(1)
    half = wbf_ref.shape[1] // 2
    group = V7X_MXU_DIM

    def w_copy(step, slot):
        cols = pl.ds(pl.multiple_of((step // n_experts) * 2 * half, 2 * half), 2 * half)
        return pltpu.make_async_copy(w_hbm.at[layer, step % n_experts, :, cols],
                                     wbuf.at[slot], wsem.at[slot])

    w_ref = _stream_weights(c * n_experts + e, pl.num_programs(0) * n_experts, w_copy, wbuf)

    def in_copies(t, slot):
        return [pltpu.make_async_copy(x_hbm.at[t], xbuf.at[slot], xsem.at[slot])]

    def out_copy(t, slot):
        return pltpu.make_async_copy(obuf.at[slot], o_hbm.at[c, t], osem.at[slot])

    def convert():
        src = lax.broadcasted_iota(jnp.int32, (group, group), 0)
        dst = lax.broadcasted_iota(jnp.int32, (group, group), 1)
        want = jnp.where(dst < group // 2, 2 * dst, 2 * (dst - group // 2) + 1)
        perm = jnp.where(src == want, 1.0, 0.0).astype(BF16)
        for gi in range(wbf_ref.shape[1] // group):
            blk = w_ref[:, gi * group:(gi + 1) * group].astype(BF16)
            split = jnp.dot(blk, perm, preferred_element_type=F32).astype(BF16)
            lo = gi * (group // 2)
            wbf_ref[:, lo:lo + group // 2] = split[:, :group // 2]
            wbf_ref[:, half + lo:half + lo + group // 2] = split[:, group // 2:]

    def step(slot):
        u = jnp.dot(xbuf[slot], wbf_ref[...], preferred_element_type=F32)
        glu = jnp.minimum(u[:, :half] + bg_ref[...], SWIGLU_LIMIT)
        lin = jnp.clip(u[:, half:] + bl_ref[...], -SWIGLU_LIMIT, SWIGLU_LIMIT)
        act = glu * jax.nn.sigmoid(SWIGLU_ALPHA * glu) * (lin + 1.0)
        obuf[slot] = act.astype(obuf.dtype)

    _run_expert_tiles(ts_ref[e], ts_ref[e + 1], in_copies, out_copy, convert, step)

    @pl.when(e == pl.num_programs(1) - 1)
    def _():
        _zero_tail_tiles(ts_ref[e + 1], n_tiles, obuf, out_copy)


def _moe_down_kernel(ts_ref, a_hbm, w_hbm, b_ref, o_hbm, wbuf, wbf_ref, abuf, ybuf, wsem, asem,
                     ysem, *, tm, n_tiles, lines, layer):
    e = pl.program_id(0)
    tf = a_hbm.shape[3]

    def w_copy(step, slot):
        return pltpu.make_async_copy(w_hbm.at[layer, step], wbuf.at[slot], wsem.at[slot])

    w_ref = _stream_weights(e, pl.num_programs(0), w_copy, wbuf)

    def in_copies(t, slot):
        return [pltpu.make_async_copy(a_hbm.at[cc, t],
                                      abuf.at[slot, :, pl.ds(cc * tf, tf)], asem.at[slot])
                for cc in range(a_hbm.shape[0])]

    def out_copy(t, slot):
        return pltpu.make_async_copy(ybuf.at[slot], o_hbm.at[t], ysem.at[slot])

    def convert():
        wbf_ref[...] = w_ref[...].astype(BF16)

    def step(slot):
        y = jnp.dot(abuf[slot], wbf_ref[...], preferred_element_type=F32) + b_ref[...]
        _store_row_contiguous(ybuf.at[slot], y)

    _run_expert_tiles(ts_ref[e], ts_ref[e + 1], in_copies, out_copy, convert, step)

    @pl.when(e == pl.num_programs(0) - 1)
    def _():
        _zero_tail_tiles(ts_ref[e + 1], n_tiles, ybuf, out_copy)


def _moe_experts(x_sorted, tile_start, w1, b1g, b1l, w2, b2, layer, tm):
    p, d = x_sorted.shape
    n_experts, f = w2.shape[1], w2.shape[2]
    n_tiles = p // tm
    tf = _tile(f, 1024)
    n_chunks = f // tf
    lines = d // V7X_LANES
    hbm = pl.BlockSpec(memory_space=pl.ANY)

    up_spec = pltpu.PrefetchScalarGridSpec(
        num_scalar_prefetch=1,
        grid=(n_chunks, n_experts),
        in_specs=[hbm, hbm,
                  pl.BlockSpec((None, None, 1, tf), lambda c, e, ts: (layer, e, 0, c)),
                  pl.BlockSpec((None, None, 1, tf), lambda c, e, ts: (layer, e, 0, c))],
        out_specs=hbm,
        scratch_shapes=[pltpu.VMEM((2, d, 2 * tf), F32), pltpu.VMEM((d, 2 * tf), BF16),
                        pltpu.VMEM((2, tm, d), BF16), pltpu.VMEM((2, tm, tf), BF16),
                        pltpu.SemaphoreType.DMA((2,)), pltpu.SemaphoreType.DMA((2,)),
                        pltpu.SemaphoreType.DMA((2,))],
    )
    act = pl.pallas_call(
        functools.partial(_moe_up_kernel, tm=tm, n_tiles=n_tiles, layer=layer),
        grid_spec=up_spec,
        out_shape=jax.ShapeDtypeStruct((n_chunks, n_tiles, tm, tf), BF16),
        compiler_params=_params(("arbitrary", "arbitrary"), 56),
        name="moe_up",
    )(tile_start, x_sorted.reshape(n_tiles, tm, d), w1, b1g, b1l)

    down_spec = pltpu.PrefetchScalarGridSpec(
        num_scalar_prefetch=1,
        grid=(n_experts,),
        in_specs=[hbm, hbm,
                  pl.BlockSpec((None, None, 1, d), lambda e, ts: (layer, e, 0, 0))],
        out_specs=hbm,
        scratch_shapes=[pltpu.VMEM((2, f, d), F32), pltpu.VMEM((f, d), BF16),
                        pltpu.VMEM((2, tm, f), BF16), pltpu.VMEM((2, tm * lines, V7X_LANES), F32),
                        pltpu.SemaphoreType.DMA((2,)), pltpu.SemaphoreType.DMA((2,)),
                        pltpu.SemaphoreType.DMA((2,))],
    )
    y = pl.pallas_call(
        functools.partial(_moe_down_kernel, tm=tm, n_tiles=n_tiles, lines=lines, layer=layer),
        grid_spec=down_spec,
        out_shape=jax.ShapeDtypeStruct((n_tiles, tm * lines, V7X_LANES), F32),
        compiler_params=_params(("arbitrary",), 56),
        name="moe_down",
    )(tile_start, act, w2, b2)
    return y.reshape(p * lines, V7X_LANES)


def _combine_kernel(*refs, tc, final_norm):
    if final_norm:
        cur_ref, nxt_ref, y_hbm, h_ref, w_ref, g_ref, fg_ref, o_ref, buf, mix_ref, sem = refs
    else:
        cur_ref, nxt_ref, y_hbm, h_ref, w_ref, g_ref, o_ref, buf, mix_ref, sem = refs
    i = pl.program_id(0)
    n = pl.num_programs(0)
    rows = TOP_K * tc
    lines = y_hbm.shape[1]

    @pl.when(i == 0)
    def _():
        _start_rows(y_hbm, cur_ref, rows, buf, 0, sem)

    @pl.when(i + 1 < n)
    def _():
        _start_rows(y_hbm, nxt_ref, rows, buf, (i + 1) % 2, sem)

    slot = i % 2
    _wait_rows(y_hbm, rows, buf, slot, sem)

    w = w_ref[...]
    blk = tc * lines
    mix = w[:, 0:1] * buf[slot, pl.ds(0, blk), :]
    for k in range(1, TOP_K):
        mix = mix + w[:, k:k + 1] * buf[slot, pl.ds(k * blk, blk), :]
    mix_ref[...] = mix
    out = h_ref[...] + g_ref[...] * _load_row_contiguous(mix_ref, 0, tc, lines)
    if final_norm:
        out = _rms(out, fg_ref[...])
    o_ref[...] = out


def _moe_combine(y_rows, pos, top_w, h, gate, seq, final_g=None):
    t, d = h.shape
    lines = d // V7X_LANES
    tc = _tile(seq, 128)
    n = t // tc
    per_batch = seq // tc
    pos3 = pos.reshape(n, tc, TOP_K).transpose(0, 2, 1).reshape(n, 1, TOP_K * tc)
    w_lines = jnp.repeat(top_w, lines, axis=0)
    smem = functools.partial(pl.BlockSpec, memory_space=pltpu.SMEM)
    final_norm = final_g is not None
    in_specs = [smem((1, 1, TOP_K * tc), lambda i: (i, 0, 0)),
                smem((1, 1, TOP_K * tc), lambda i: (jnp.minimum(i + 1, n - 1), 0, 0)),
                pl.BlockSpec(memory_space=pl.ANY),
                pl.BlockSpec((tc, d), lambda i: (i, 0)),
                pl.BlockSpec((tc * lines, TOP_K), lambda i: (i, 0)),
                pl.BlockSpec((None, 1, d), lambda i: (i // per_batch, 0, 0))]
    args = [pos3, pos3, y_rows, h, w_lines, gate]
    if final_norm:
        in_specs.append(pl.BlockSpec((1, d), lambda i: (0, 0)))
        args.append(final_g.reshape(1, d))
    return pl.pallas_call(
        functools.partial(_combine_kernel, tc=tc, final_norm=final_norm),
        grid=(n,),
        in_specs=in_specs,
        out_specs=pl.BlockSpec((tc, d), lambda i: (i, 0)),
        out_shape=jax.ShapeDtypeStruct((t, d), F32),
        scratch_shapes=[pltpu.VMEM((2, TOP_K * tc * lines, V7X_LANES), F32),
                        pltpu.VMEM((tc * lines, V7X_LANES), F32),
                        pltpu.SemaphoreType.DMA((2,))],
        compiler_params=_params(("arbitrary",), 32),
        name="moe_combine",
    )(*args)


def _moe_layer(h, layer, seq, norm_g, scale, shift, gate, router_w, router_b,
               moe_w1, moe_b1, moe_w2, moe_b2, final_g=None):
    n_experts = router_w.shape[-1]
    d = h.shape[1]
    lanes = V7X_LANES * pl.cdiv(n_experts, V7X_LANES)
    rw_pad = jnp.pad(router_w[layer], ((0, 0), (0, lanes - n_experts)))
    rb_pad = jnp.pad(router_b[layer], (0, lanes - n_experts),
                     constant_values=NEG_BIG).reshape(1, lanes)
    f_in, top_idx, top_w = _moe_route(h, norm_g, scale, shift, rw_pad, rb_pad, seq)
    tm = 256
    src_tok, pos, tile_start, n_used = _routing_tables(top_idx, n_experts, tm)
    lines = d // V7X_LANES
    x_sorted = _moe_dispatch(f_in.reshape(-1, lines, V7X_LANES), src_tok, n_used, tm)
    f2 = moe_b1.shape[-1]
    b1 = moe_b1.reshape(moe_b1.shape[0], n_experts, f2 // 2, 2)
    b1g = b1[..., 0].reshape(moe_b1.shape[0], n_experts, 1, f2 // 2)
    b1l = b1[..., 1].reshape(moe_b1.shape[0], n_experts, 1, f2 // 2)
    b2 = moe_b2.reshape(moe_b2.shape[0], n_experts, 1, d)
    y_sorted = _moe_experts(x_sorted, tile_start, moe_w1, b1g, b1l, moe_w2, b2, layer, tm)
    return _moe_combine(y_sorted.reshape(-1, lines, V7X_LANES), pos, top_w, h, gate, seq,
                        final_g)


def _mla_weight_layouts(w_in, w_q_up, w_kv_up, heads, q_lora, kv_lora):
    half = ROPE_DIM // 2
    pad = HEAD_DIM - ROPE_DIM
    d = w_in.shape[0]
    rope = w_in[:, q_lora + kv_lora:]
    w_in_p = jnp.concatenate(
        [w_in[:, :q_lora + kv_lora], rope[:, 0::2], rope[:, 1::2], jnp.zeros((d, pad), F32)],
        axis=1).astype(BF16)
    wq = w_q_up.reshape(q_lora, heads, HEAD_DIM + ROPE_DIM)
    wq_rope = wq[:, :, HEAD_DIM:]
    wq_p = jnp.concatenate(
        [wq[:, :, :HEAD_DIM], wq_rope[:, :, 0::2], wq_rope[:, :, 1::2],
         jnp.zeros((q_lora, heads, pad), F32)], axis=2)
    wq_p = wq_p.reshape(q_lora, heads * 2 * HEAD_DIM).astype(BF16)
    wkv = w_kv_up.reshape(kv_lora, heads, 2 * HEAD_DIM)
    wkv_p = jnp.concatenate([wkv[:, :, :HEAD_DIM].reshape(kv_lora, heads * HEAD_DIM),
                             wkv[:, :, HEAD_DIM:].reshape(kv_lora, heads * HEAD_DIM)],
                            axis=1).astype(BF16)
    assert half * 2 == ROPE_DIM and w_in_p.shape[1] == q_lora + kv_lora + HEAD_DIM
    return w_in_p, wq_p, wkv_p


def kernel(x, c, positions, mod_w, mod_b, attn_norm_g, ffn_norm_g, mla_w_in, mla_q_norm_g,
           mla_kv_norm_g, mla_w_q_up, mla_w_kv_up, mla_w_out, shared_norm_g, shared_w_kvf,
           shared_b_f, fox_w_q, fox_w_out, router_w, router_b, moe_w1, moe_b1, moe_w2, moe_b2,
           final_norm_g):
    b, s, d = x.shape
    depth = mod_w.shape[0]
    n_a = mla_w_in.shape[0]
    heads = d // HEAD_DIM
    q_lora = mla_q_norm_g.shape[-1]
    kv_lora = mla_kv_norm_g.shape[-1]
    hd = heads * HEAD_DIM
    t = b * s

    c_pad = jnp.pad(c, ((0, (-b) % V7X_SUBLANES), (0, 0)))
    mod = _modulation(c_pad, mod_w, mod_b)[:, :b, :]
    tables = _rope_tables(positions)

    h = x.reshape(t, d)
    kv_sh = fq = fk = None
    for layer in range(depth):
        sh_a, sc_a, g_a, sh_f, sc_f, g_f = (
            mod[layer, :, i * d:(i + 1) * d].reshape(b, 1, d) for i in range(N_MOD))
        if layer < n_a:
            w_in_p, wq_p, wkv_p = _mla_weight_layouts(
                mla_w_in[layer], mla_w_q_up[layer], mla_w_kv_up[layer], heads, q_lora, kv_lora)
            proj = _norm_matmul(h, attn_norm_g[layer], w_in_p, s, scale=sc_a, shift=sh_a)
            q, k, v = _mla_qkv(proj, mla_q_norm_g[layer], mla_kv_norm_g[layer], wq_p, wkv_p,
                               tables, heads, q_lora, kv_lora,
                               LOG2E * (HEAD_DIM + ROPE_DIM) ** -0.5)
            o = _flash_attention(q.reshape(b, s, -1), k.reshape(b, s, -1),
                                 v.reshape(b, s, -1), heads, HEAD_DIM)
            w_out = mla_w_out[layer].astype(BF16)
        else:
            j = layer - n_a
            q = _norm_matmul(h, attn_norm_g[layer], fox_w_q[j].astype(BF16), s, scale=sc_a,
                             shift=sh_a, out_dtype=BF16,
                             out_scale=LOG2E * HEAD_DIM ** -0.5)
            o = _flash_attention(q.reshape(b, s, hd), kv_sh, kv_sh, heads, HEAD_DIM,
                                 v_block0=heads, fq=fq, fk=fk)
            w_out = fox_w_out[j].astype(BF16)
        h = _out_proj_residual(o.reshape(t, hd), w_out, h, g_a, s)
        last = layer == depth - 1
        h = _moe_layer(h, layer, s, ffn_norm_g[layer], sc_f, sh_f, g_f, router_w, router_b,
                       moe_w1, moe_b1, moe_w2, moe_b2, final_norm_g if last else None)
        if layer == n_a - 1:
            kv_sh = _norm_matmul(h, shared_norm_g, shared_w_kvf[:, :2 * hd].astype(BF16), s,
                                 out_dtype=BF16).reshape(b, s, 2 * hd)
            w_f = jnp.pad(shared_w_kvf[:, 2 * hd:], ((0, 0), (0, V7X_LANES - heads)))
            f_logit = _norm_matmul(h, shared_norm_g, w_f, s)
            b_pad = jnp.pad(shared_b_f, (0, V7X_LANES - heads)).reshape(1, V7X_LANES)
            fq = _forget_cumsum(f_logit.reshape(b, s, V7X_LANES), b_pad, LOG2E)
            fk = fq[:, :, :heads].transpose(0, 2, 1).reshape(b, heads, 1, s)
    return h.reshape(b, s, d)
```

```python
import functools

import jax
import jax.numpy as jnp
from jax import lax
from jax.experimental import pallas as pl
from jax.experimental.pallas import tpu as pltpu

F32 = jnp.float32
BF16 = jnp.bfloat16
HIGHEST = lax.Precision.HIGHEST

RMS_EPS = 1e-6
HEAD_DIM = 128
ROPE_DIM = 64
ROPE_THETA = 10000.0
TOP_K = 4
SWIGLU_ALPHA = 1.702
SWIGLU_LIMIT = 7.0
N_MOD = 6

V7X_LANES = 128
V7X_SUBLANES = 8
V7X_MXU_DIM = 256
V7X_VMEM_BYTES = 64 * 1024 * 1024
MIB = 1024 * 1024

NEG_BIG = -1e30


def _params(semantics, vmem_mib):
    assert vmem_mib * MIB < V7X_VMEM_BYTES
    return pltpu.CompilerParams(dimension_semantics=semantics,
                                vmem_limit_bytes=vmem_mib * MIB)


def _tile(n, pref):
    if n <= pref:
        return n
    t = pref
    while n % t:
        t //= 2
    return t


def _rms(x, g):
    ms = jnp.mean(x * x, axis=-1, keepdims=True)
    return x * lax.rsqrt(ms + RMS_EPS) * g


def _store_row_contiguous(ref, x):
    rows, d = x.shape
    lines = d // V7X_LANES
    for s in range(lines):
        ref[pl.ds(s, rows, stride=lines), :] = x[:, s * V7X_LANES:(s + 1) * V7X_LANES]


def _load_row_contiguous(ref, base, rows, lines):
    return jnp.concatenate(
        [ref[pl.ds(base + s, rows, stride=lines), :] for s in range(lines)], axis=1)


def _pack_bf16_pairs(x):
    half = x.shape[1] // 2
    hi = lax.bitcast_convert_type(x[:, :half].astype(BF16).astype(F32), jnp.uint32)
    lo = lax.bitcast_convert_type(x[:, half:].astype(BF16).astype(F32), jnp.uint32)
    return hi | (lo >> 16)


def _unpack_bf16_pairs(words):
    hi = lax.bitcast_convert_type(words & jnp.uint32(0xFFFF0000), F32)
    lo = lax.bitcast_convert_type(words << 16, F32)
    return hi, lo


def _packed_lines(d):
    return d // (2 * V7X_LANES)


def _mod_kernel(c_ref, w_ref, b_ref, o_ref):
    o_ref[...] = jnp.dot(c_ref[...], w_ref[...], precision=HIGHEST,
                         preferred_element_type=F32) + b_ref[...]


def _modulation(c_pad, mod_w, mod_b):
    n_layers, d, n = mod_w.shape
    rows = c_pad.shape[0]
    tn = _tile(n, 1024)
    return pl.pallas_call(
        _mod_kernel,
        grid=(n_layers, n // tn),
        in_specs=[
            pl.BlockSpec((rows, d), lambda l, j: (0, 0)),
            pl.BlockSpec((None, d, tn), lambda l, j: (l, 0, j)),
            pl.BlockSpec((None, 1, tn), lambda l, j: (l, 0, j)),
        ],
        out_specs=pl.BlockSpec((None, rows, tn), lambda l, j: (l, 0, j)),
        out_shape=jax.ShapeDtypeStruct((n_layers, rows, n), F32),
        compiler_params=_params(("arbitrary", "arbitrary"), 40),
        name="modulation",
    )(c_pad, mod_w, mod_b.reshape(n_layers, 1, n))


def _norm_matmul_kernel(*refs, modulate, out_scale):
    if modulate:
        h_ref, g_ref, sc_ref, sh_ref, w_ref, o_ref = refs
    else:
        h_ref, g_ref, w_ref, o_ref = refs
    y = _rms(h_ref[...], g_ref[...])
    if modulate:
        y = y * (1.0 + sc_ref[...]) + sh_ref[...]
    w = w_ref[...]
    if w.dtype == BF16:
        out = jnp.dot(y.astype(BF16), w, preferred_element_type=F32)
    else:
        out = jnp.dot(y, w, precision=HIGHEST, preferred_element_type=F32)
    if out_scale != 1.0:
        out = out * out_scale
    o_ref[...] = out.astype(o_ref.dtype)


def _norm_matmul(h, g, w, seq, *, scale=None, shift=None, out_dtype=F32, out_scale=1.0):
    t, d = h.shape
    n = w.shape[1]
    tm = _tile(seq, 256)
    tn = _tile(n, 2048)
    per_batch = seq // tm
    modulate = scale is not None
    in_specs = [pl.BlockSpec((tm, d), lambda j, i: (i, 0)),
                pl.BlockSpec((1, d), lambda j, i: (0, 0))]
    args = [h, g.reshape(1, d)]
    if modulate:
        mod_spec = pl.BlockSpec((None, 1, d), lambda j, i: (i // per_batch, 0, 0))
        in_specs += [mod_spec, mod_spec]
        args += [scale, shift]
    in_specs.append(pl.BlockSpec((d, tn), lambda j, i: (0, j)))
    args.append(w)
    return pl.pallas_call(
        functools.partial(_norm_matmul_kernel, modulate=modulate, out_scale=out_scale),
        grid=(n // tn, t // tm),
        in_specs=in_specs,
        out_specs=pl.BlockSpec((tm, tn), lambda j, i: (i, j)),
        out_shape=jax.ShapeDtypeStruct((t, n), out_dtype),
        compiler_params=_params(("arbitrary", "arbitrary"), 48),
        name="norm_matmul",
    )(*args)


def _rope_table_kernel(pos_ref, freq_ref, c_ref, s1_ref, s2_ref):
    half = ROPE_DIM // 2
    ang = pos_ref[...].astype(F32) * freq_ref[...]
    lane = lax.broadcasted_iota(jnp.int32, ang.shape, 1)
    cos = jnp.cos(ang)
    sin = jnp.sin(ang)
    c_ref[...] = jnp.where(lane < 2 * half, cos, 0.0)
    s1_ref[...] = jnp.where(lane < half, -sin, 0.0)
    s2_ref[...] = jnp.where((lane >= half) & (lane < 2 * half), sin, 0.0)


def _rope_tables(positions):
    t = positions.size
    half = ROPE_DIM // 2
    inv_freq = ROPE_THETA ** (-jnp.arange(0, ROPE_DIM, 2, dtype=F32) / ROPE_DIM)
    freq = jnp.tile(inv_freq, V7X_LANES // half).reshape(1, V7X_LANES)
    tm = _tile(t, 512)
    spec = pl.BlockSpec((tm, V7X_LANES), lambda i: (i, 0))
    shape = jax.ShapeDtypeStruct((t, V7X_LANES), F32)
    return pl.pallas_call(
        _rope_table_kernel,
        grid=(t // tm,),
        in_specs=[pl.BlockSpec((tm, 1), lambda i: (i, 0)),
                  pl.BlockSpec((1, V7X_LANES), lambda i: (0, 0))],
        out_specs=[spec, spec, spec],
        out_shape=[shape, shape, shape],
        compiler_params=_params(("arbitrary",), 16),
        name="rope_tables",
    )(positions.reshape(t, 1), freq)


def _rope_chunk(z, c, s1, s2):
    half = ROPE_DIM // 2
    return (z * c + pltpu.roll(z, V7X_LANES - half, axis=1) * s1
            + pltpu.roll(z, half, axis=1) * s2)


def _mla_q_kernel(x_ref, g_ref, w_ref, c_ref, s1_ref, s2_ref, o_ref, *, heads, scale):
    y = _rms(x_ref[...], g_ref[...]).astype(BF16)
    q = jnp.dot(y, w_ref[...], preferred_element_type=F32) * scale
    c, s1, s2 = c_ref[...], s1_ref[...], s2_ref[...]
    for h in range(heads):
        base = h * 2 * HEAD_DIM
        o_ref[:, base:base + HEAD_DIM] = q[:, base:base + HEAD_DIM].astype(BF16)
        z = q[:, base + HEAD_DIM:base + 2 * HEAD_DIM]
        o_ref[:, base + HEAD_DIM:base + 2 * HEAD_DIM] = _rope_chunk(z, c, s1, s2).astype(BF16)


def _mla_kv_kernel(x_ref, r_ref, g_ref, w_ref, c_ref, s1_ref, s2_ref, k_ref, v_ref, *, heads):
    y = _rms(x_ref[...], g_ref[...]).astype(BF16)
    kv = jnp.dot(y, w_ref[...], preferred_element_type=F32)
    kr = _rope_chunk(r_ref[...], c_ref[...], s1_ref[...], s2_ref[...]).astype(BF16)
    for h in range(heads):
        base = h * 2 * HEAD_DIM
        k_ref[:, base:base + HEAD_DIM] = kv[:, h * HEAD_DIM:(h + 1) * HEAD_DIM].astype(BF16)
        k_ref[:, base + HEAD_DIM:base + 2 * HEAD_DIM] = kr
    v_ref[...] = kv[:, heads * HEAD_DIM:].astype(BF16)


def _mla_qkv(proj, q_norm_g, kv_norm_g, wq, wkv, tables, heads, q_lora, kv_lora, scale):
    t = proj.shape[0]
    tm = _tile(t, 256)
    dq = heads * 2 * HEAD_DIM
    assert q_lora == kv_lora and q_lora % V7X_LANES == 0
    tab_spec = pl.BlockSpec((tm, V7X_LANES), lambda i: (i, 0))
    rope_block = (q_lora + kv_lora) // V7X_LANES
    q = pl.pallas_call(
        functools.partial(_mla_q_kernel, heads=heads, scale=scale),
        grid=(t // tm,),
        in_specs=[pl.BlockSpec((tm, q_lora), lambda i: (i, 0)),
                  pl.BlockSpec((1, q_lora), lambda i: (0, 0)),
                  pl.BlockSpec((q_lora, dq), lambda i: (0, 0)),
                  tab_spec, tab_spec, tab_spec],
        out_specs=pl.BlockSpec((tm, dq), lambda i: (i, 0)),
        out_shape=jax.ShapeDtypeStruct((t, dq), BF16),
        compiler_params=_params(("arbitrary",), 40),
        name="mla_q",
    )(proj, q_norm_g.reshape(1, q_lora), wq, *tables)
    k, v = pl.pallas_call(
        functools.partial(_mla_kv_kernel, heads=heads),
        grid=(t // tm,),
        in_specs=[pl.BlockSpec((tm, kv_lora), lambda i: (i, 1)),
                  pl.BlockSpec((tm, V7X_LANES), lambda i: (i, rope_block)),
                  pl.BlockSpec((1, kv_lora), lambda i: (0, 0)),
                  pl.BlockSpec((kv_lora, dq), lambda i: (0, 0)),
                  tab_spec, tab_spec, tab_spec],
        out_specs=[pl.BlockSpec((tm, dq), lambda i: (i, 0)),
                   pl.BlockSpec((tm, heads * HEAD_DIM), lambda i: (i, 0))],
        out_shape=[jax.ShapeDtypeStruct((t, dq), BF16),
                   jax.ShapeDtypeStruct((t, heads * HEAD_DIM), BF16)],
        compiler_params=_params(("arbitrary",), 40),
        name="mla_kv",
    )(proj, proj, kv_norm_g.reshape(1, kv_lora), wkv, *tables)
    return q, k, v


FLASH_HEADS_PER_STEP = 4
LOG2E = 1.4426950408889634


def _flash_kernel(*refs, tq, dk, dv, hp, decay):
    if decay:
        q_ref, k_ref, v_ref, fq_ref, fk_ref, o_ref = refs
    else:
        q_ref, k_ref, v_ref, o_ref = refs
    qi = pl.program_id(2)
    qs = [q_ref[:, a * dk:(a + 1) * dk] for a in range(hp)]
    if decay:
        fq_all = fq_ref[...]
        lane = lax.broadcasted_iota(jnp.int32, fq_all.shape, 1)
        fqs = [jnp.sum(jnp.where(lane == pl.program_id(1) * hp + a, fq_all, 0.0), axis=-1,
                       keepdims=True) for a in range(hp)]

    def block(j, carry, masked):
        start = pl.multiple_of(j * tq, tq)
        out = []
        for a in range(hp):
            m, l, acc = carry[a]
            k = k_ref[pl.ds(start, tq), a * dk:(a + 1) * dk]
            v = v_ref[pl.ds(start, tq), a * dv:(a + 1) * dv]
            s = lax.dot_general(qs[a], k, (((1,), (1,)), ((), ())),
                                preferred_element_type=F32)
            if decay:
                s = s + (fqs[a] - fk_ref[a, :, pl.ds(start, tq)])
            if masked:
                row = lax.broadcasted_iota(jnp.int32, s.shape, 0)
                col = lax.broadcasted_iota(jnp.int32, s.shape, 1)
                s = jnp.where(col <= row, s, -jnp.inf)
            m_new = jnp.maximum(m, jnp.max(s, axis=-1, keepdims=True))
            p = jnp.exp2(s - m_new)
            alpha = jnp.exp2(m - m_new)
            l = alpha * l + jnp.sum(p, axis=-1, keepdims=True)
            acc = alpha * acc + jnp.dot(p.astype(BF16), v, preferred_element_type=F32)
            out.append((m_new, l, acc))
        return tuple(out)

    init = tuple((jnp.full((tq, 1), -jnp.inf, F32), jnp.zeros((tq, 1), F32),
                  jnp.zeros((tq, dv), F32)) for _ in range(hp))
    carry = lax.fori_loop(0, qi, lambda j, c: block(j, c, False), init)
    final = block(qi, carry, True)
    for a in range(hp):
        _, l, acc = final[a]
        o_ref[:, a * dv:(a + 1) * dv] = (acc / l).astype(o_ref.dtype)


def _flash_attention(q, k, v, heads, dv, v_block0=0, fq=None, fk=None):
    b, s, _ = q.shape
    dk = q.shape[-1] // heads
    tq = _tile(s, 512)
    hp = min(FLASH_HEADS_PER_STEP, heads)
    assert heads % hp == 0 and v_block0 % hp == 0
    vb0 = v_block0 // hp
    decay = fq is not None
    in_specs = [pl.BlockSpec((None, tq, hp * dk), lambda bi, h, i: (bi, i, h)),
                pl.BlockSpec((None, s, hp * dk), lambda bi, h, i: (bi, 0, h)),
                pl.BlockSpec((None, s, hp * dv), lambda bi, h, i: (bi, 0, vb0 + h))]
    args = [q, k, v]
    if decay:
        in_specs += [pl.BlockSpec((None, tq, V7X_LANES), lambda bi, h, i: (bi, i, 0)),
                     pl.BlockSpec((None, hp, 1, s), lambda bi, h, i: (bi, h, 0, 0))]
        args += [fq, fk]
    return pl.pallas_call(
        functools.partial(_flash_kernel, tq=tq, dk=dk, dv=dv, hp=hp, decay=decay),
        grid=(b, heads // hp, s // tq),
        in_specs=in_specs,
        out_specs=pl.BlockSpec((None, tq, hp * dv), lambda bi, h, i: (bi, i, h)),
        out_shape=jax.ShapeDtypeStruct((b, s, heads * dv), BF16),
        compiler_params=_params(("arbitrary", "arbitrary", "arbitrary"), 40),
        name="flash_attention",
    )(*args)


def _out_proj_kernel(o_ref, w_ref, h_ref, g_ref, out_ref):
    a = jnp.dot(o_ref[...], w_ref[...], preferred_element_type=F32)
    out_ref[...] = h_ref[...] + g_ref[...] * a


def _out_proj_residual(o, w, h, gate, seq):
    t, dk = o.shape
    d = w.shape[1]
    tm = _tile(seq, 256)
    per_batch = seq // tm
    return pl.pallas_call(
        _out_proj_kernel,
        grid=(t // tm,),
        in_specs=[pl.BlockSpec((tm, dk), lambda i: (i, 0)),
                  pl.BlockSpec((dk, d), lambda i: (0, 0)),
                  pl.BlockSpec((tm, d), lambda i: (i, 0)),
                  pl.BlockSpec((None, 1, d), lambda i: (i // per_batch, 0, 0))],
        out_specs=pl.BlockSpec((tm, d), lambda i: (i, 0)),
        out_shape=jax.ShapeDtypeStruct((t, d), F32),
        compiler_params=_params(("arbitrary",), 40),
        name="out_proj_residual",
    )(o, w, h, gate)


def _forget_cumsum_kernel(f_ref, b_ref, o_ref, carry_ref, *, out_scale):
    @pl.when(pl.program_id(1) == 0)
    def _():
        carry_ref[...] = jnp.zeros_like(carry_ref)

    x = f_ref[...] + b_ref[...]
    log_f = jnp.minimum(x, 0.0) - jnp.log1p(jnp.exp(-jnp.abs(x)))
    ts = x.shape[0]
    row = lax.broadcasted_iota(jnp.int32, (ts, ts), 0)
    col = lax.broadcasted_iota(jnp.int32, (ts, ts), 1)
    tri = jnp.where(col <= row, 1.0, 0.0).astype(F32)
    c = jnp.dot(tri, log_f, precision=HIGHEST, preferred_element_type=F32) + carry_ref[...]
    o_ref[...] = c * out_scale
    carry_ref[...] = c[ts - 1:ts, :]


def _forget_cumsum(f_logit, b_pad, out_scale):
    b, s, lanes = f_logit.shape
    ts = _tile(s, 512)
    return pl.pallas_call(
        functools.partial(_forget_cumsum_kernel, out_scale=out_scale),
        grid=(b, s // ts),
        in_specs=[pl.BlockSpec((None, ts, lanes), lambda bi, i: (bi, i, 0)),
                  pl.BlockSpec((1, lanes), lambda bi, i: (0, 0))],
        out_specs=pl.BlockSpec((None, ts, lanes), lambda bi, i: (bi, i, 0)),
        out_shape=jax.ShapeDtypeStruct((b, s, lanes), F32),
        scratch_shapes=[pltpu.VMEM((1, lanes), F32)],
        compiler_params=_params(("arbitrary", "arbitrary"), 16),
        name="forget_cumsum",
    )(f_logit, b_pad)


def _moe_route_kernel(h_ref, g_ref, sc_ref, sh_ref, rw_ref, rb_ref, f_ref, idx_ref, w_ref):
    y = _rms(h_ref[...], g_ref[...]) * (1.0 + sc_ref[...]) + sh_ref[...]
    _store_row_contiguous(f_ref, _pack_bf16_pairs(y))
    logits = jnp.dot(y, rw_ref[...], precision=HIGHEST,
                     preferred_element_type=F32) + rb_ref[...]
    lane = lax.broadcasted_iota(jnp.int32, logits.shape, 1).astype(F32)
    vals, idxs = [], []
    for _ in range(TOP_K):
        m = jnp.max(logits, axis=-1, keepdims=True)
        ix = jnp.min(jnp.where(logits == m, lane, float(logits.shape[1])), axis=-1,
                     keepdims=True)
        vals.append(m)
        idxs.append(ix)
        logits = jnp.where(lane == ix, -jnp.inf, logits)
    exps = [jnp.exp(v - vals[0]) for v in vals]
    den = exps[0]
    for e in exps[1:]:
        den = den + e
    slot = lax.broadcasted_iota(jnp.int32, idx_ref.shape, 1)
    idx_out = jnp.zeros(idx_ref.shape, jnp.int32)
    w_out = jnp.zeros(w_ref.shape, F32)
    for k in range(TOP_K):
        idx_out = jnp.where(slot == k, idxs[k].astype(jnp.int32), idx_out)
        w_out = jnp.where(slot == k, exps[k] / den, w_out)
    idx_ref[...] = idx_out
    w_ref[...] = w_out


def _moe_route(h, g, scale, shift, rw_pad, rb_pad, seq):
    t, d = h.shape
    tm = _tile(seq, 256)
    per_batch = seq // tm
    lanes = rw_pad.shape[1]
    lines = _packed_lines(d)
    mod_spec = pl.BlockSpec((None, 1, d), lambda i: (i // per_batch, 0, 0))
    return pl.pallas_call(
        _moe_route_kernel,
        grid=(t // tm,),
        in_specs=[pl.BlockSpec((tm, d), lambda i: (i, 0)),
                  pl.BlockSpec((1, d), lambda i: (0, 0)),
                  mod_spec, mod_spec,
                  pl.BlockSpec((d, lanes), lambda i: (0, 0)),
                  pl.BlockSpec((1, lanes), lambda i: (0, 0))],
        out_specs=[pl.BlockSpec((tm * lines, V7X_LANES), lambda i: (i, 0)),
                   pl.BlockSpec((tm, TOP_K), lambda i: (i, 0)),
                   pl.BlockSpec((tm, TOP_K), lambda i: (i, 0))],
        out_shape=[jax.ShapeDtypeStruct((t * lines, V7X_LANES), jnp.uint32),
                   jax.ShapeDtypeStruct((t, TOP_K), jnp.int32),
                   jax.ShapeDtypeStruct((t, TOP_K), F32)],
        compiler_params=_params(("arbitrary",), 32),
        name="moe_route",
    )(h, g.reshape(1, d), scale, shift, rw_pad, rb_pad)


def _routing_tables(top_idx, n_experts, tm):
    t, k = top_idx.shape
    n_tiles = (t * k) // tm + n_experts
    e_flat = top_idx.reshape(-1)
    onehot = (e_flat[:, None] == jnp.arange(n_experts, dtype=jnp.int32)[None, :]).astype(jnp.int32)
    incl = jnp.cumsum(onehot, axis=0)
    counts = incl[-1]
    rank = jnp.sum((incl - onehot) * onehot, axis=1)
    tiles_per = (counts + tm - 1) // tm
    tile_end = jnp.cumsum(tiles_per)
    tile_start = jnp.concatenate([jnp.zeros((1,), jnp.int32), tile_end]).astype(jnp.int32)
    row_start = tile_start[:-1] * tm
    pos = row_start[e_flat] + rank
    n_used = tile_end[-1]
    order = jnp.argsort(e_flat, stable=True).astype(jnp.int32)
    count_start = jnp.cumsum(counts) - counts
    rows = jnp.arange(n_tiles * tm, dtype=jnp.int32)
    row_expert = jnp.sum((rows[:, None] // tm >= tile_end[None, :]).astype(jnp.int32), axis=1)
    row_expert = jnp.minimum(row_expert, n_experts - 1)
    r = rows - row_start[row_expert]
    valid = r < counts[row_expert]
    src_flat = order[jnp.clip(count_start[row_expert] + r, 0, t * k - 1)]
    src_tok = jnp.where(valid, src_flat // k, 0)
    return (src_tok.astype(jnp.int32), pos.reshape(t, k), tile_start,
            n_used.reshape(1).astype(jnp.int32))


ROW_DMA_UNROLL = 8


def _row_copy(src_hbm, row, buf, slot, r, sem):
    lines = src_hbm.shape[1]
    dst = buf.at[slot, pl.ds(pl.multiple_of(r * lines, lines), lines), :]
    return pltpu.make_async_copy(src_hbm.at[row], dst, sem.at[slot])


def _start_rows(src_hbm, idx_ref, n, buf, slot, sem):
    def body(i, c):
        for prio in range(2):
            r = 2 * i + prio
            _row_copy(src_hbm, idx_ref[0, 0, r], buf, slot, r, sem).start(priority=prio)
        return c
    assert n % 2 == 0
    lax.fori_loop(0, n // 2, body, 0, unroll=ROW_DMA_UNROLL // 2)


def _wait_rows(src_hbm, n, buf, slot, sem):
    def body(r, c):
        _row_copy(src_hbm, 0, buf, slot, r, sem).wait()
        return c
    lax.fori_loop(0, n, body, 0, unroll=ROW_DMA_UNROLL)


def _dispatch_kernel(nt_ref, cur_ref, nxt_ref, f_hbm, o_ref, buf, sem, *, tm):
    i = pl.program_id(0)
    nt = nt_ref[0]

    @pl.when(i == 0)
    def _():
        _start_rows(f_hbm, cur_ref, tm, buf, 0, sem)

    @pl.when(i + 1 < nt)
    def _():
        _start_rows(f_hbm, nxt_ref, tm, buf, (i + 1) % 2, sem)

    @pl.when(i < nt)
    def _():
        slot = i % 2
        _wait_rows(f_hbm, tm, buf, slot, sem)
        hi, lo = _unpack_bf16_pairs(_load_row_contiguous(buf.at[slot], 0, tm, f_hbm.shape[1]))
        half = hi.shape[1]
        o_ref[:, :half] = hi.astype(o_ref.dtype)
        o_ref[:, half:] = lo.astype(o_ref.dtype)

    @pl.when(i >= nt)
    def _():
        o_ref[...] = jnp.zeros_like(o_ref)


def _moe_dispatch(f_rows, src_tok, n_used, tm):
    t, lines, lanes = f_rows.shape
    d = 2 * lines * lanes
    n_tiles = src_tok.shape[0] // tm
    src3 = src_tok.reshape(n_tiles, 1, tm)
    smem = functools.partial(pl.BlockSpec, memory_space=pltpu.SMEM)
    grid_spec = pltpu.PrefetchScalarGridSpec(
        num_scalar_prefetch=1,
        grid=(n_tiles,),
        in_specs=[smem((1, 1, tm), lambda i, nt: (i, 0, 0)),
                  smem((1, 1, tm), lambda i, nt: (jnp.minimum(i + 1, n_tiles - 1), 0, 0)),
                  pl.BlockSpec(memory_space=pl.ANY)],
        out_specs=pl.BlockSpec((tm, d), lambda i, nt: (i, 0)),
        scratch_shapes=[pltpu.VMEM((2, tm * lines, lanes), jnp.uint32),
                        pltpu.SemaphoreType.DMA((2,))],
    )
    return pl.pallas_call(
        functools.partial(_dispatch_kernel, tm=tm),
        grid_spec=grid_spec,
        out_shape=jax.ShapeDtypeStruct((n_tiles * tm, d), BF16),
        compiler_params=_params(("arbitrary",), 32),
        name="moe_dispatch",
    )(n_used, src3, src3, f_rows)


TILE_DMA_PRIORITY = 0


def _run_expert_tiles(t0, t1, in_copies, out_copy, prologue, step):
    @pl.when(t1 > t0)
    def _():
        for cp in in_copies(t0, 0):
            cp.start(priority=TILE_DMA_PRIORITY)
        prologue()

        def body(t, c):
            slot = (t - t0) % 2
            for cp in in_copies(t, slot):
                cp.wait()

            @pl.when(t + 1 < t1)
            def _():
                for cp in in_copies(t + 1, 1 - slot):
                    cp.start(priority=TILE_DMA_PRIORITY)

            @pl.when(t - t0 >= 2)
            def _():
                out_copy(t - 2, slot).wait()

            step(slot)
            out_copy(t, slot).start(priority=TILE_DMA_PRIORITY)
            return c
        lax.fori_loop(t0, t1, body, 0)

        @pl.when(t1 - t0 >= 2)
        def _():
            out_copy(t1 - 2, (t1 - 2 - t0) % 2).wait()
        out_copy(t1 - 1, (t1 - 1 - t0) % 2).wait()


def _zero_tail_tiles(first, n_tiles, obuf, out_copy):
    obuf[0] = jnp.zeros(obuf.shape[1:], obuf.dtype)

    def body(t, c):
        out_copy(t, 0).start()
        out_copy(t, 0).wait()
        return c
    lax.fori_loop(first, n_tiles, body, 0)


WEIGHT_DMA_PRIORITY = 1


def _stream_weights(step, n_steps, copy_for_step, wbuf):
    slot = step % 2

    @pl.when(step == 0)
    def _():
        copy_for_step(step, 0).start(priority=WEIGHT_DMA_PRIORITY)

    @pl.when(step + 1 < n_steps)
    def _():
        copy_for_step(step + 1, 1 - slot).start(priority=WEIGHT_DMA_PRIORITY)

    copy_for_step(step, slot).wait()
    return wbuf.at[slot]


def _moe_up_kernel(ts_ref, x_hbm, w_hbm, bg_ref, bl_ref, o_hbm, wbuf, wbf_ref, xbuf, obuf,
                   wsem, xsem, osem, *, tm, n_tiles, layer):
    c = pl.program_id(0)
    e = pl.program_id(1)
    n_experts = pl.num_programs(1)
    half = wbf_ref.shape[1] // 2
    group = V7X_MXU_DIM

    def w_copy(step, slot):
        cols = pl.ds(pl.multiple_of((step // n_experts) * 2 * half, 2 * half), 2 * half)
        return pltpu.make_async_copy(w_hbm.at[layer, step % n_experts, :, cols],
                                     wbuf.at[slot], wsem.at[slot])

    w_ref = _stream_weights(c * n_experts + e, pl.num_programs(0) * n_experts, w_copy, wbuf)

    def in_copies(t, slot):
        return [pltpu.make_async_copy(x_hbm.at[t], xbuf.at[slot], xsem.at[slot])]

    def out_copy(t, slot):
        return pltpu.make_async_copy(obuf.at[slot], o_hbm.at[c, t], osem.at[slot])

    def convert():
        src = lax.broadcasted_iota(jnp.int32, (group, group), 0)
        dst = lax.broadcasted_iota(jnp.int32, (group, group), 1)
        want = jnp.where(dst < group // 2, 2 * dst, 2 * (dst - group // 2) + 1)
        perm = jnp.where(src == want, 1.0, 0.0).astype(BF16)
        for gi in range(wbf_ref.shape[1] // group):
            blk = w_ref[:, gi * group:(gi + 1) * group].astype(BF16)
            split = jnp.dot(blk, perm, preferred_element_type=F32).astype(BF16)
            lo = gi * (group // 2)
            wbf_ref[:, lo:lo + group // 2] = split[:, :group // 2]
            wbf_ref[:, half + lo:half + lo + group // 2] = split[:, group // 2:]

    def step(slot):
        u = jnp.dot(xbuf[slot], wbf_ref[...], preferred_element_type=F32)
        glu = jnp.minimum(u[:, :half] + bg_ref[...], SWIGLU_LIMIT)
        lin = jnp.clip(u[:, half:] + bl_ref[...], -SWIGLU_LIMIT, SWIGLU_LIMIT)
        act = glu * jax.nn.sigmoid(SWIGLU_ALPHA * glu) * (lin + 1.0)
        obuf[slot] = act.astype(obuf.dtype)

    _run_expert_tiles(ts_ref[e], ts_ref[e + 1], in_copies, out_copy, convert, step)

    @pl.when(e == pl.num_programs(1) - 1)
    def _():
        _zero_tail_tiles(ts_ref[e + 1], n_tiles, obuf, out_copy)


def _moe_down_kernel(ts_ref, a_hbm, w_hbm, b_ref, o_hbm, wbuf, wbf_ref, abuf, ybuf, wsem, asem,
                     ysem, *, tm, n_tiles, lines, layer):
    e = pl.program_id(0)
    tf = a_hbm.shape[3]

    def w_copy(step, slot):
        return pltpu.make_async_copy(w_hbm.at[layer, step], wbuf.at[slot], wsem.at[slot])

    w_ref = _stream_weights(e, pl.num_programs(0), w_copy, wbuf)

    def in_copies(t, slot):
        return [pltpu.make_async_copy(a_hbm.at[cc, t],
                                      abuf.at[slot, :, pl.ds(cc * tf, tf)], asem.at[slot])
                for cc in range(a_hbm.shape[0])]

    def out_copy(t, slot):
        return pltpu.make_async_copy(ybuf.at[slot], o_hbm.at[t], ysem.at[slot])

    def convert():
        wbf_ref[...] = w_ref[...].astype(BF16)

    def step(slot):
        y = jnp.dot(abuf[slot], wbf_ref[...], preferred_element_type=F32) + b_ref[...]
        _store_row_contiguous(ybuf.at[slot], _pack_bf16_pairs(y))

    _run_expert_tiles(ts_ref[e], ts_ref[e + 1], in_copies, out_copy, convert, step)

    @pl.when(e == pl.num_programs(0) - 1)
    def _():
        _zero_tail_tiles(ts_ref[e + 1], n_tiles, ybuf, out_copy)


def _moe_experts(x_sorted, tile_start, w1, b1g, b1l, w2, b2, layer, tm):
    p, d = x_sorted.shape
    n_experts, f = w2.shape[1], w2.shape[2]
    n_tiles = p // tm
    tf = _tile(f, 1024)
    n_chunks = f // tf
    lines = _packed_lines(d)
    hbm = pl.BlockSpec(memory_space=pl.ANY)

    up_spec = pltpu.PrefetchScalarGridSpec(
        num_scalar_prefetch=1,
        grid=(n_chunks, n_experts),
        in_specs=[hbm, hbm,
                  pl.BlockSpec((None, None, 1, tf), lambda c, e, ts: (layer, e, 0, c)),
                  pl.BlockSpec((None, None, 1, tf), lambda c, e, ts: (layer, e, 0, c))],
        out_specs=hbm,
        scratch_shapes=[pltpu.VMEM((2, d, 2 * tf), F32), pltpu.VMEM((d, 2 * tf), BF16),
                        pltpu.VMEM((2, tm, d), BF16), pltpu.VMEM((2, tm, tf), BF16),
                        pltpu.SemaphoreType.DMA((2,)), pltpu.SemaphoreType.DMA((2,)),
                        pltpu.SemaphoreType.DMA((2,))],
    )
    act = pl.pallas_call(
        functools.partial(_moe_up_kernel, tm=tm, n_tiles=n_tiles, layer=layer),
        grid_spec=up_spec,
        out_shape=jax.ShapeDtypeStruct((n_chunks, n_tiles, tm, tf), BF16),
        compiler_params=_params(("arbitrary", "arbitrary"), 56),
        name="moe_up",
    )(tile_start, x_sorted.reshape(n_tiles, tm, d), w1, b1g, b1l)

    down_spec = pltpu.PrefetchScalarGridSpec(
        num_scalar_prefetch=1,
        grid=(n_experts,),
        in_specs=[hbm, hbm,
                  pl.BlockSpec((None, None, 1, d), lambda e, ts: (layer, e, 0, 0))],
        out_specs=hbm,
        scratch_shapes=[pltpu.VMEM((2, f, d), F32), pltpu.VMEM((f, d), BF16),
                        pltpu.VMEM((2, tm, f), BF16),
                        pltpu.VMEM((2, tm * lines, V7X_LANES), jnp.uint32),
                        pltpu.SemaphoreType.DMA((2,)), pltpu.SemaphoreType.DMA((2,)),
                        pltpu.SemaphoreType.DMA((2,))],
    )
    y = pl.pallas_call(
        functools.partial(_moe_down_kernel, tm=tm, n_tiles=n_tiles, lines=lines, layer=layer),
        grid_spec=down_spec,
        out_shape=jax.ShapeDtypeStruct((n_tiles, tm * lines, V7X_LANES), jnp.uint32),
        compiler_params=_params(("arbitrary",), 56),
        name="moe_down",
    )(tile_start, act, w2, b2)
    return y.reshape(p * lines, V7X_LANES)


def _combine_kernel(*refs, tc, final_norm):
    if final_norm:
        cur_ref, nxt_ref, y_hbm, h_ref, w_ref, g_ref, fg_ref, o_ref, buf, mix_ref, sem = refs
    else:
        cur_ref, nxt_ref, y_hbm, h_ref, w_ref, g_ref, o_ref, buf, mix_ref, sem = refs
    i = pl.program_id(0)
    n = pl.num_programs(0)
    rows = TOP_K * tc
    lines = y_hbm.shape[1]

    @pl.when(i == 0)
    def _():
        _start_rows(y_hbm, cur_ref, rows, buf, 0, sem)

    @pl.when(i + 1 < n)
    def _():
        _start_rows(y_hbm, nxt_ref, rows, buf, (i + 1) % 2, sem)

    slot = i % 2
    _wait_rows(y_hbm, rows, buf, slot, sem)

    w = w_ref[...]
    blk = tc * lines
    mix_hi = mix_lo = None
    for k in range(TOP_K):
        hi, lo = _unpack_bf16_pairs(buf[slot, pl.ds(k * blk, blk), :])
        wk = w[:, k:k + 1]
        mix_hi = wk * hi if mix_hi is None else mix_hi + wk * hi
        mix_lo = wk * lo if mix_lo is None else mix_lo + wk * lo
    mix_ref[0] = mix_hi
    mix_ref[1] = mix_lo
    mixed = jnp.concatenate([_load_row_contiguous(mix_ref.at[0], 0, tc, lines),
                             _load_row_contiguous(mix_ref.at[1], 0, tc, lines)], axis=1)
    out = h_ref[...] + g_ref[...] * mixed
    if final_norm:
        out = _rms(out, fg_ref[...])
    o_ref[...] = out


def _moe_combine(y_rows, pos, top_w, h, gate, seq, final_g=None):
    t, d = h.shape
    lines = _packed_lines(d)
    tc = _tile(seq, 128)
    n = t // tc
    per_batch = seq // tc
    pos3 = pos.reshape(n, tc, TOP_K).transpose(0, 2, 1).reshape(n, 1, TOP_K * tc)
    w_lines = jnp.repeat(top_w, lines, axis=0)
    smem = functools.partial(pl.BlockSpec, memory_space=pltpu.SMEM)
    final_norm = final_g is not None
    in_specs = [smem((1, 1, TOP_K * tc), lambda i: (i, 0, 0)),
                smem((1, 1, TOP_K * tc), lambda i: (jnp.minimum(i + 1, n - 1), 0, 0)),
                pl.BlockSpec(memory_space=pl.ANY),
                pl.BlockSpec((tc, d), lambda i: (i, 0)),
                pl.BlockSpec((tc * lines, TOP_K), lambda i: (i, 0)),
                pl.BlockSpec((None, 1, d), lambda i: (i // per_batch, 0, 0))]
    args = [pos3, pos3, y_rows, h, w_lines, gate]
    if final_norm:
        in_specs.append(pl.BlockSpec((1, d), lambda i: (0, 0)))
        args.append(final_g.reshape(1, d))
    return pl.pallas_call(
        functools.partial(_combine_kernel, tc=tc, final_norm=final_norm),
        grid=(n,),
        in_specs=in_specs,
        out_specs=pl.BlockSpec((tc, d), lambda i: (i, 0)),
        out_shape=jax.ShapeDtypeStruct((t, d), F32),
        scratch_shapes=[pltpu.VMEM((2, TOP_K * tc * lines, V7X_LANES), jnp.uint32),
                        pltpu.VMEM((2, tc * lines, V7X_LANES), F32),
                        pltpu.SemaphoreType.DMA((2,))],
        compiler_params=_params(("arbitrary",), 32),
        name="moe_combine",
    )(*args)


def _moe_layer(h, layer, seq, norm_g, scale, shift, gate, router_w, router_b,
               moe_w1, moe_b1, moe_w2, moe_b2, final_g=None):
    n_experts = router_w.shape[-1]
    d = h.shape[1]
    lanes = V7X_LANES * pl.cdiv(n_experts, V7X_LANES)
    rw_pad = jnp.pad(router_w[layer], ((0, 0), (0, lanes - n_experts)))
    rb_pad = jnp.pad(router_b[layer], (0, lanes - n_experts),
                     constant_values=NEG_BIG).reshape(1, lanes)
    f_in, top_idx, top_w = _moe_route(h, norm_g, scale, shift, rw_pad, rb_pad, seq)
    tm = 256
    src_tok, pos, tile_start, n_used = _routing_tables(top_idx, n_experts, tm)
    lines = _packed_lines(d)
    x_sorted = _moe_dispatch(f_in.reshape(-1, lines, V7X_LANES), src_tok, n_used, tm)
    f2 = moe_b1.shape[-1]
    b1 = moe_b1.reshape(moe_b1.shape[0], n_experts, f2 // 2, 2)
    b1g = b1[..., 0].reshape(moe_b1.shape[0], n_experts, 1, f2 // 2)
    b1l = b1[..., 1].reshape(moe_b1.shape[0], n_experts, 1, f2 // 2)
    b2 = moe_b2.reshape(moe_b2.shape[0], n_experts, 1, d)
    y_sorted = _moe_experts(x_sorted, tile_start, moe_w1, b1g, b1l, moe_w2, b2, layer, tm)
    return _moe_combine(y_sorted.reshape(-1, lines, V7X_LANES), pos, top_w, h, gate, seq,
                        final_g)


def _mla_weight_layouts(w_in, w_q_up, w_kv_up, heads, q_lora, kv_lora):
    half = ROPE_DIM // 2
    pad = HEAD_DIM - ROPE_DIM
    d = w_in.shape[0]
    rope = w_in[:, q_lora + kv_lora:]
    w_in_p = jnp.concatenate(
        [w_in[:, :q_lora + kv_lora], rope[:, 0::2], rope[:, 1::2], jnp.zeros((d, pad), F32)],
        axis=1).astype(BF16)
    wq = w_q_up.reshape(q_lora, heads, HEAD_DIM + ROPE_DIM)
    wq_rope = wq[:, :, HEAD_DIM:]
    wq_p = jnp.concatenate(
        [wq[:, :, :HEAD_DIM], wq_rope[:, :, 0::2], wq_rope[:, :, 1::2],
         jnp.zeros((q_lora, heads, pad), F32)], axis=2)
    wq_p = wq_p.reshape(q_lora, heads * 2 * HEAD_DIM).astype(BF16)
    wkv = w_kv_up.reshape(kv_lora, heads, 2 * HEAD_DIM)
    wkv_p = jnp.concatenate([wkv[:, :, :HEAD_DIM].reshape(kv_lora, heads * HEAD_DIM),
                             wkv[:, :, HEAD_DIM:].reshape(kv_lora, heads * HEAD_DIM)],
                            axis=1).astype(BF16)
    assert half * 2 == ROPE_DIM and w_in_p.shape[1] == q_lora + kv_lora + HEAD_DIM
    return w_in_p, wq_p, wkv_p


def kernel(x, c, positions, mod_w, mod_b, attn_norm_g, ffn_norm_g, mla_w_in, mla_q_norm_g,
           mla_kv_norm_g, mla_w_q_up, mla_w_kv_up, mla_w_out, shared_norm_g, shared_w_kvf,
           shared_b_f, fox_w_q, fox_w_out, router_w, router_b, moe_w1, moe_b1, moe_w2, moe_b2,
           final_norm_g):
    b, s, d = x.shape
    depth = mod_w.shape[0]
    n_a = mla_w_in.shape[0]
    heads = d // HEAD_DIM
    q_lora = mla_q_norm_g.shape[-1]
    kv_lora = mla_kv_norm_g.shape[-1]
    hd = heads * HEAD_DIM
    t = b * s

    c_pad = jnp.pad(c, ((0, (-b) % V7X_SUBLANES), (0, 0)))
    mod = _modulation(c_pad, mod_w, mod_b)[:, :b, :]
    tables = _rope_tables(positions)

    h = x.reshape(t, d)
    kv_sh = fq = fk = None
    for layer in range(depth):
        sh_a, sc_a, g_a, sh_f, sc_f, g_f = (
            mod[layer, :, i * d:(i + 1) * d].reshape(b, 1, d) for i in range(N_MOD))
        if layer < n_a:
            w_in_p, wq_p, wkv_p = _mla_weight_layouts(
                mla_w_in[layer], mla_w_q_up[layer], mla_w_kv_up[layer], heads, q_lora, kv_lora)
            proj = _norm_matmul(h, attn_norm_g[layer], w_in_p, s, scale=sc_a, shift=sh_a)
            q, k, v = _mla_qkv(proj, mla_q_norm_g[layer], mla_kv_norm_g[layer], wq_p, wkv_p,
                               tables, heads, q_lora, kv_lora,
                               LOG2E * (HEAD_DIM + ROPE_DIM) ** -0.5)
            o = _flash_attention(q.reshape(b, s, -1), k.reshape(b, s, -1),
                                 v.reshape(b, s, -1), heads, HEAD_DIM)
            w_out = mla_w_out[layer].astype(BF16)
        else:
            j = layer - n_a
            q = _norm_matmul(h, attn_norm_g[layer], fox_w_q[j].astype(BF16), s, scale=sc_a,
                             shift=sh_a, out_dtype=BF16,
                             out_scale=LOG2E * HEAD_DIM ** -0.5)
            o = _flash_attention(q.reshape(b, s, hd), kv_sh, kv_sh, heads, HEAD_DIM,
                                 v_block0=heads, fq=fq, fk=fk)
            w_out = fox_w_out[j].astype(BF16)
        h = _out_proj_residual(o.reshape(t, hd), w_out, h, g_a, s)
        last = layer == depth - 1
        h = _moe_layer(h, layer, s, ffn_norm_g[layer], sc_f, sh_f, g_f, router_w, router_b,
                       moe_w1, moe_b1, moe_w2, moe_b2, final_norm_g if last else None)
        if layer == n_a - 1:
            kv_sh = _norm_matmul(h, shared_norm_g, shared_w_kvf[:, :2 * hd].astype(BF16), s,
                                 out_dtype=BF16).reshape(b, s, 2 * hd)
            w_f = jnp.pad(shared_w_kvf[:, 2 * hd:], ((0, 0), (0, V7X_LANES - heads)))
            f_logit = _norm_matmul(h, shared_norm_g, w_f, s)
            b_pad = jnp.pad(shared_b_f, (0, V7X_LANES - heads)).reshape(1, V7X_LANES)
            fq = _forget_cumsum(f_logit.reshape(b, s, V7X_LANES), b_pad, LOG2E)
            fk = fq[:, :, :heads].transpose(0, 2, 1).reshape(b, heads, 1, s)
    return h.reshape(b, s, d)
```

```python
import functools

import jax
import jax.numpy as jnp
from jax import lax
from jax.experimental import pallas as pl
from jax.experimental.pallas import tpu as pltpu

F32 = jnp.float32
BF16 = jnp.bfloat16
HIGHEST = lax.Precision.HIGHEST

RMS_EPS = 1e-6
HEAD_DIM = 128
ROPE_DIM = 64
ROPE_THETA = 10000.0
TOP_K = 4
SWIGLU_ALPHA = 1.702
SWIGLU_LIMIT = 7.0
N_MOD = 6

V7X_LANES = 128
V7X_SUBLANES = 8
V7X_MXU_DIM = 256
V7X_VMEM_BYTES = 64 * 1024 * 1024
MIB = 1024 * 1024

NEG_BIG = -1e30


def _params(semantics, vmem_mib):
    assert vmem_mib * MIB < V7X_VMEM_BYTES
    return pltpu.CompilerParams(dimension_semantics=semantics,
                                vmem_limit_bytes=vmem_mib * MIB)


def _tile(n, pref):
    if n <= pref:
        return n
    t = pref
    while n % t:
        t //= 2
    return t


def _rms(x, g):
    ms = jnp.mean(x * x, axis=-1, keepdims=True)
    return x * lax.rsqrt(ms + RMS_EPS) * g


def _store_row_contiguous(ref, x):
    rows, d = x.shape
    lines = d // V7X_LANES
    for s in range(lines):
        ref[pl.ds(s, rows, stride=lines), :] = x[:, s * V7X_LANES:(s + 1) * V7X_LANES]


def _load_row_contiguous(ref, base, rows, lines):
    return jnp.concatenate(
        [ref[pl.ds(base + s, rows, stride=lines), :] for s in range(lines)], axis=1)


def _pack_bf16_pairs(x):
    half = x.shape[1] // 2
    hi = lax.bitcast_convert_type(x[:, :half].astype(BF16).astype(F32), jnp.uint32)
    lo = lax.bitcast_convert_type(x[:, half:].astype(BF16).astype(F32), jnp.uint32)
    return hi | (lo >> 16)


def _unpack_bf16_pairs(words):
    hi = lax.bitcast_convert_type(words & jnp.uint32(0xFFFF0000), F32)
    lo = lax.bitcast_convert_type(words << 16, F32)
    return hi, lo


def _packed_lines(d):
    return d // (2 * V7X_LANES)


def _mod_kernel(c_ref, w_ref, b_ref, o_ref):
    o_ref[...] = jnp.dot(c_ref[...], w_ref[...], precision=HIGHEST,
                         preferred_element_type=F32) + b_ref[...]


def _modulation(c_pad, mod_w, mod_b):
    n_layers, d, n = mod_w.shape
    rows = c_pad.shape[0]
    tn = _tile(n, 1024)
    return pl.pallas_call(
        _mod_kernel,
        grid=(n_layers, n // tn),
        in_specs=[
            pl.BlockSpec((rows, d), lambda l, j: (0, 0)),
            pl.BlockSpec((None, d, tn), lambda l, j: (l, 0, j)),
            pl.BlockSpec((None, 1, tn), lambda l, j: (l, 0, j)),
        ],
        out_specs=pl.BlockSpec((None, rows, tn), lambda l, j: (l, 0, j)),
        out_shape=jax.ShapeDtypeStruct((n_layers, rows, n), F32),
        compiler_params=_params(("arbitrary", "arbitrary"), 40),
        name="modulation",
    )(c_pad, mod_w, mod_b.reshape(n_layers, 1, n))


def _norm_matmul_kernel(*refs, modulate, side, out_scale):
    refs = list(refs)
    h_ref, g_ref = refs[:2]
    sc_ref, sh_ref = refs[2:4] if modulate else (None, None)
    w_ref = refs[4 if modulate else 2]
    ws_ref = refs[-3] if side else None
    o_ref, os_ref = (refs[-2], refs[-1]) if side else (refs[-1], None)
    y = _rms(h_ref[...], g_ref[...])
    if modulate:
        y = y * (1.0 + sc_ref[...]) + sh_ref[...]
    w = w_ref[...]
    if w.dtype == BF16:
        out = jnp.dot(y.astype(BF16), w, preferred_element_type=F32)
    else:
        out = jnp.dot(y, w, precision=HIGHEST, preferred_element_type=F32)
    if out_scale != 1.0:
        out = out * out_scale
    o_ref[...] = out.astype(o_ref.dtype)
    if side:
        os_ref[...] = jnp.dot(y, ws_ref[...], precision=HIGHEST, preferred_element_type=F32)


def _norm_matmul(h, g, w, seq, *, scale=None, shift=None, out_dtype=F32, out_scale=1.0,
                 side_w=None):
    t, d = h.shape
    n = w.shape[1]
    side = side_w is not None
    tm = _tile(seq, 256)
    tn = n if side else _tile(n, 2048)
    per_batch = seq // tm
    modulate = scale is not None
    in_specs = [pl.BlockSpec((tm, d), lambda j, i: (i, 0)),
                pl.BlockSpec((1, d), lambda j, i: (0, 0))]
    args = [h, g.reshape(1, d)]
    if modulate:
        mod_spec = pl.BlockSpec((None, 1, d), lambda j, i: (i // per_batch, 0, 0))
        in_specs += [mod_spec, mod_spec]
        args += [scale, shift]
    in_specs.append(pl.BlockSpec((d, tn), lambda j, i: (0, j)))
    args.append(w)
    out_specs = pl.BlockSpec((tm, tn), lambda j, i: (i, j))
    out_shape = jax.ShapeDtypeStruct((t, n), out_dtype)
    if side:
        ns = side_w.shape[1]
        in_specs.append(pl.BlockSpec((d, ns), lambda j, i: (0, 0)))
        args.append(side_w)
        out_specs = [out_specs, pl.BlockSpec((tm, ns), lambda j, i: (i, 0))]
        out_shape = [out_shape, jax.ShapeDtypeStruct((t, ns), F32)]
    return pl.pallas_call(
        functools.partial(_norm_matmul_kernel, modulate=modulate, side=side,
                          out_scale=out_scale),
        grid=(n // tn, t // tm),
        in_specs=in_specs,
        out_specs=out_specs,
        out_shape=out_shape,
        compiler_params=_params(("arbitrary", "arbitrary"), 56 if side else 48),
        name="norm_matmul",
    )(*args)


def _rope_table_kernel(pos_ref, freq_ref, c_ref, s1_ref, s2_ref):
    half = ROPE_DIM // 2
    ang = pos_ref[...].astype(F32) * freq_ref[...]
    lane = lax.broadcasted_iota(jnp.int32, ang.shape, 1)
    cos = jnp.cos(ang)
    sin = jnp.sin(ang)
    c_ref[...] = jnp.where(lane < 2 * half, cos, 0.0)
    s1_ref[...] = jnp.where(lane < half, -sin, 0.0)
    s2_ref[...] = jnp.where((lane >= half) & (lane < 2 * half), sin, 0.0)


def _rope_tables(positions):
    t = positions.size
    half = ROPE_DIM // 2
    inv_freq = ROPE_THETA ** (-jnp.arange(0, ROPE_DIM, 2, dtype=F32) / ROPE_DIM)
    freq = jnp.tile(inv_freq, V7X_LANES // half).reshape(1, V7X_LANES)
    tm = _tile(t, 512)
    spec = pl.BlockSpec((tm, V7X_LANES), lambda i: (i, 0))
    shape = jax.ShapeDtypeStruct((t, V7X_LANES), F32)
    return pl.pallas_call(
        _rope_table_kernel,
        grid=(t // tm,),
        in_specs=[pl.BlockSpec((tm, 1), lambda i: (i, 0)),
                  pl.BlockSpec((1, V7X_LANES), lambda i: (0, 0))],
        out_specs=[spec, spec, spec],
        out_shape=[shape, shape, shape],
        compiler_params=_params(("arbitrary",), 16),
        name="rope_tables",
    )(positions.reshape(t, 1), freq)


def _rope_chunk(z, c, s1, s2):
    half = ROPE_DIM // 2
    return (z * c + pltpu.roll(z, V7X_LANES - half, axis=1) * s1
            + pltpu.roll(z, half, axis=1) * s2)


def _mla_q_kernel(x_ref, g_ref, w_ref, c_ref, s1_ref, s2_ref, o_ref, *, heads, scale):
    y = _rms(x_ref[...], g_ref[...]).astype(BF16)
    q = jnp.dot(y, w_ref[...], preferred_element_type=F32) * scale
    c, s1, s2 = c_ref[...], s1_ref[...], s2_ref[...]
    for h in range(heads):
        base = h * 2 * HEAD_DIM
        o_ref[:, base:base + HEAD_DIM] = q[:, base:base + HEAD_DIM].astype(BF16)
        z = q[:, base + HEAD_DIM:base + 2 * HEAD_DIM]
        o_ref[:, base + HEAD_DIM:base + 2 * HEAD_DIM] = _rope_chunk(z, c, s1, s2).astype(BF16)


def _mla_kv_kernel(x_ref, r_ref, g_ref, w_ref, c_ref, s1_ref, s2_ref, k_ref, v_ref, *, heads):
    y = _rms(x_ref[...], g_ref[...]).astype(BF16)
    kv = jnp.dot(y, w_ref[...], preferred_element_type=F32)
    kr = _rope_chunk(r_ref[...], c_ref[...], s1_ref[...], s2_ref[...]).astype(BF16)
    for h in range(heads):
        base = h * 2 * HEAD_DIM
        k_ref[:, base:base + HEAD_DIM] = kv[:, h * HEAD_DIM:(h + 1) * HEAD_DIM].astype(BF16)
        k_ref[:, base + HEAD_DIM:base + 2 * HEAD_DIM] = kr
    v_ref[...] = kv[:, heads * HEAD_DIM:].astype(BF16)


def _mla_qkv(proj, q_norm_g, kv_norm_g, wq, wkv, tables, heads, q_lora, kv_lora, scale):
    t = proj.shape[0]
    tm = _tile(t, 256)
    dq = heads * 2 * HEAD_DIM
    assert q_lora == kv_lora and q_lora % V7X_LANES == 0
    tab_spec = pl.BlockSpec((tm, V7X_LANES), lambda i: (i, 0))
    rope_block = (q_lora + kv_lora) // V7X_LANES
    q = pl.pallas_call(
        functools.partial(_mla_q_kernel, heads=heads, scale=scale),
        grid=(t // tm,),
        in_specs=[pl.BlockSpec((tm, q_lora), lambda i: (i, 0)),
                  pl.BlockSpec((1, q_lora), lambda i: (0, 0)),
                  pl.BlockSpec((q_lora, dq), lambda i: (0, 0)),
                  tab_spec, tab_spec, tab_spec],
        out_specs=pl.BlockSpec((tm, dq), lambda i: (i, 0)),
        out_shape=jax.ShapeDtypeStruct((t, dq), BF16),
        compiler_params=_params(("arbitrary",), 40),
        name="mla_q",
    )(proj, q_norm_g.reshape(1, q_lora), wq, *tables)
    k, v = pl.pallas_call(
        functools.partial(_mla_kv_kernel, heads=heads),
        grid=(t // tm,),
        in_specs=[pl.BlockSpec((tm, kv_lora), lambda i: (i, 1)),
                  pl.BlockSpec((tm, V7X_LANES), lambda i: (i, rope_block)),
                  pl.BlockSpec((1, kv_lora), lambda i: (0, 0)),
                  pl.BlockSpec((kv_lora, dq), lambda i: (0, 0)),
                  tab_spec, tab_spec, tab_spec],
        out_specs=[pl.BlockSpec((tm, dq), lambda i: (i, 0)),
                   pl.BlockSpec((tm, heads * HEAD_DIM), lambda i: (i, 0))],
        out_shape=[jax.ShapeDtypeStruct((t, dq), BF16),
                   jax.ShapeDtypeStruct((t, heads * HEAD_DIM), BF16)],
        compiler_params=_params(("arbitrary",), 40),
        name="mla_kv",
    )(proj, proj, kv_norm_g.reshape(1, kv_lora), wkv, *tables)
    return q, k, v


FLASH_HEADS_PER_STEP = 4
LOG2E = 1.4426950408889634


def _flash_kernel(*refs, tq, dk, dv, hp, decay):
    if decay:
        q_ref, k_ref, v_ref, fq_ref, fk_ref, o_ref = refs
    else:
        q_ref, k_ref, v_ref, o_ref = refs
    qi = pl.program_id(2)
    qs = [q_ref[:, a * dk:(a + 1) * dk] for a in range(hp)]
    if decay:
        fq_all = fq_ref[...]
        lane = lax.broadcasted_iota(jnp.int32, fq_all.shape, 1)
        fqs = [jnp.sum(jnp.where(lane == pl.program_id(1) * hp + a, fq_all, 0.0), axis=-1,
                       keepdims=True) for a in range(hp)]

    def block(j, carry, masked):
        start = pl.multiple_of(j * tq, tq)
        out = []
        for a in range(hp):
            m, l, acc = carry[a]
            k = k_ref[pl.ds(start, tq), a * dk:(a + 1) * dk]
            v = v_ref[pl.ds(start, tq), a * dv:(a + 1) * dv]
            s = lax.dot_general(qs[a], k, (((1,), (1,)), ((), ())),
                                preferred_element_type=F32)
            if decay:
                s = s + (fqs[a] - fk_ref[a, :, pl.ds(start, tq)])
            if masked:
                row = lax.broadcasted_iota(jnp.int32, s.shape, 0)
                col = lax.broadcasted_iota(jnp.int32, s.shape, 1)
                s = jnp.where(col <= row, s, -jnp.inf)
            m_new = jnp.maximum(m, jnp.max(s, axis=-1, keepdims=True))
            p = jnp.exp2(s - m_new)
            alpha = jnp.exp2(m - m_new)
            l = alpha * l + jnp.sum(p, axis=-1, keepdims=True)
            acc = alpha * acc + jnp.dot(p.astype(BF16), v, preferred_element_type=F32)
            out.append((m_new, l, acc))
        return tuple(out)

    init = tuple((jnp.full((tq, 1), -jnp.inf, F32), jnp.zeros((tq, 1), F32),
                  jnp.zeros((tq, dv), F32)) for _ in range(hp))
    carry = lax.fori_loop(0, qi, lambda j, c: block(j, c, False), init)
    final = block(qi, carry, True)
    for a in range(hp):
        _, l, acc = final[a]
        o_ref[:, a * dv:(a + 1) * dv] = (acc / l).astype(o_ref.dtype)


def _flash_attention(q, k, v, heads, dv, v_block0=0, fq=None, fk=None):
    b, s, _ = q.shape
    dk = q.shape[-1] // heads
    tq = _tile(s, 512)
    hp = min(FLASH_HEADS_PER_STEP, heads)
    assert heads % hp == 0 and v_block0 % hp == 0
    vb0 = v_block0 // hp
    decay = fq is not None
    in_specs = [pl.BlockSpec((None, tq, hp * dk), lambda bi, h, i: (bi, i, h)),
                pl.BlockSpec((None, s, hp * dk), lambda bi, h, i: (bi, 0, h)),
                pl.BlockSpec((None, s, hp * dv), lambda bi, h, i: (bi, 0, vb0 + h))]
    args = [q, k, v]
    if decay:
        in_specs += [pl.BlockSpec((None, tq, V7X_LANES), lambda bi, h, i: (bi, i, 0)),
                     pl.BlockSpec((None, hp, 1, s), lambda bi, h, i: (bi, h, 0, 0))]
        args += [fq, fk]
    return pl.pallas_call(
        functools.partial(_flash_kernel, tq=tq, dk=dk, dv=dv, hp=hp, decay=decay),
        grid=(b, heads // hp, s // tq),
        in_specs=in_specs,
        out_specs=pl.BlockSpec((None, tq, hp * dv), lambda bi, h, i: (bi, i, h)),
        out_shape=jax.ShapeDtypeStruct((b, s, heads * dv), BF16),
        compiler_params=_params(("arbitrary", "arbitrary", "arbitrary"), 40),
        name="flash_attention",
    )(*args)


def _out_proj_kernel(o_ref, w_ref, h_ref, g_ref, out_ref):
    a = jnp.dot(o_ref[...], w_ref[...], preferred_element_type=F32)
    out_ref[...] = h_ref[...] + g_ref[...] * a


def _out_proj_residual(o, w, h, gate, seq):
    t, dk = o.shape
    d = w.shape[1]
    tm = _tile(seq, 256)
    per_batch = seq // tm
    return pl.pallas_call(
        _out_proj_kernel,
        grid=(t // tm,),
        in_specs=[pl.BlockSpec((tm, dk), lambda i: (i, 0)),
                  pl.BlockSpec((dk, d), lambda i: (0, 0)),
                  pl.BlockSpec((tm, d), lambda i: (i, 0)),
                  pl.BlockSpec((None, 1, d), lambda i: (i // per_batch, 0, 0))],
        out_specs=pl.BlockSpec((tm, d), lambda i: (i, 0)),
        out_shape=jax.ShapeDtypeStruct((t, d), F32),
        compiler_params=_params(("arbitrary",), 40),
        name="out_proj_residual",
    )(o, w, h, gate)


def _forget_cumsum_kernel(f_ref, b_ref, o_ref, carry_ref, *, out_scale):
    @pl.when(pl.program_id(1) == 0)
    def _():
        carry_ref[...] = jnp.zeros_like(carry_ref)

    x = f_ref[...] + b_ref[...]
    log_f = jnp.minimum(x, 0.0) - jnp.log1p(jnp.exp(-jnp.abs(x)))
    ts = x.shape[0]
    row = lax.broadcasted_iota(jnp.int32, (ts, ts), 0)
    col = lax.broadcasted_iota(jnp.int32, (ts, ts), 1)
    tri = jnp.where(col <= row, 1.0, 0.0).astype(F32)
    c = jnp.dot(tri, log_f, precision=HIGHEST, preferred_element_type=F32) + carry_ref[...]
    o_ref[...] = c * out_scale
    carry_ref[...] = c[ts - 1:ts, :]


def _forget_cumsum(f_logit, b_pad, out_scale):
    b, s, lanes = f_logit.shape
    ts = _tile(s, 512)
    return pl.pallas_call(
        functools.partial(_forget_cumsum_kernel, out_scale=out_scale),
        grid=(b, s // ts),
        in_specs=[pl.BlockSpec((None, ts, lanes), lambda bi, i: (bi, i, 0)),
                  pl.BlockSpec((1, lanes), lambda bi, i: (0, 0))],
        out_specs=pl.BlockSpec((None, ts, lanes), lambda bi, i: (bi, i, 0)),
        out_shape=jax.ShapeDtypeStruct((b, s, lanes), F32),
        scratch_shapes=[pltpu.VMEM((1, lanes), F32)],
        compiler_params=_params(("arbitrary", "arbitrary"), 16),
        name="forget_cumsum",
    )(f_logit, b_pad)


def _moe_route_kernel(h_ref, g_ref, sc_ref, sh_ref, rw_ref, rb_ref, f_ref, idx_ref, w_ref):
    y = _rms(h_ref[...], g_ref[...]) * (1.0 + sc_ref[...]) + sh_ref[...]
    _store_row_contiguous(f_ref, _pack_bf16_pairs(y))
    logits = jnp.dot(y, rw_ref[...], precision=HIGHEST,
                     preferred_element_type=F32) + rb_ref[...]
    lane = lax.broadcasted_iota(jnp.int32, logits.shape, 1).astype(F32)
    vals, idxs = [], []
    for _ in range(TOP_K):
        m = jnp.max(logits, axis=-1, keepdims=True)
        ix = jnp.min(jnp.where(logits == m, lane, float(logits.shape[1])), axis=-1,
                     keepdims=True)
        vals.append(m)
        idxs.append(ix)
        logits = jnp.where(lane == ix, -jnp.inf, logits)
    exps = [jnp.exp(v - vals[0]) for v in vals]
    den = exps[0]
    for e in exps[1:]:
        den = den + e
    slot = lax.broadcasted_iota(jnp.int32, idx_ref.shape, 1)
    idx_out = jnp.zeros(idx_ref.shape, jnp.int32)
    w_out = jnp.zeros(w_ref.shape, F32)
    for k in range(TOP_K):
        idx_out = jnp.where(slot == k, idxs[k].astype(jnp.int32), idx_out)
        w_out = jnp.where(slot == k, exps[k] / den, w_out)
    idx_ref[...] = idx_out
    w_ref[...] = w_out


def _moe_route(h, g, scale, shift, rw_pad, rb_pad, seq):
    t, d = h.shape
    tm = _tile(seq, 256)
    per_batch = seq // tm
    lanes = rw_pad.shape[1]
    lines = _packed_lines(d)
    mod_spec = pl.BlockSpec((None, 1, d), lambda i: (i // per_batch, 0, 0))
    return pl.pallas_call(
        _moe_route_kernel,
        grid=(t // tm,),
        in_specs=[pl.BlockSpec((tm, d), lambda i: (i, 0)),
                  pl.BlockSpec((1, d), lambda i: (0, 0)),
                  mod_spec, mod_spec,
                  pl.BlockSpec((d, lanes), lambda i: (0, 0)),
                  pl.BlockSpec((1, lanes), lambda i: (0, 0))],
        out_specs=[pl.BlockSpec((tm * lines, V7X_LANES), lambda i: (i, 0)),
                   pl.BlockSpec((tm, TOP_K), lambda i: (i, 0)),
                   pl.BlockSpec((tm, TOP_K), lambda i: (i, 0))],
        out_shape=[jax.ShapeDtypeStruct((t * lines, V7X_LANES), jnp.uint32),
                   jax.ShapeDtypeStruct((t, TOP_K), jnp.int32),
                   jax.ShapeDtypeStruct((t, TOP_K), F32)],
        compiler_params=_params(("arbitrary",), 32),
        name="moe_route",
    )(h, g.reshape(1, d), scale, shift, rw_pad, rb_pad)


def _routing_tables(top_idx, n_experts, tm):
    t, k = top_idx.shape
    n_tiles = (t * k) // tm + n_experts
    e_flat = top_idx.reshape(-1)
    onehot = (e_flat[:, None] == jnp.arange(n_experts, dtype=jnp.int32)[None, :]).astype(jnp.int32)
    counts = jnp.sum(onehot, axis=0)
    count_start = jnp.cumsum(counts) - counts
    order = jnp.argsort(e_flat, stable=True).astype(jnp.int32)
    rank = jnp.argsort(order).astype(jnp.int32) - count_start[e_flat]
    tiles_per = (counts + tm - 1) // tm
    tile_end = jnp.cumsum(tiles_per)
    tile_start = jnp.concatenate([jnp.zeros((1,), jnp.int32), tile_end]).astype(jnp.int32)
    row_start = tile_start[:-1] * tm
    pos = row_start[e_flat] + rank
    n_used = tile_end[-1]
    rows = jnp.arange(n_tiles * tm, dtype=jnp.int32)
    row_expert = jnp.sum((rows[:, None] // tm >= tile_end[None, :]).astype(jnp.int32), axis=1)
    row_expert = jnp.minimum(row_expert, n_experts - 1)
    r = rows - row_start[row_expert]
    valid = r < counts[row_expert]
    src_flat = order[jnp.clip(count_start[row_expert] + r, 0, t * k - 1)]
    src_tok = jnp.where(valid, src_flat // k, 0)
    return (src_tok.astype(jnp.int32), pos.reshape(t, k), tile_start,
            n_used.reshape(1).astype(jnp.int32))


ROW_DMA_UNROLL = 8


def _row_copy(src_hbm, row, buf, slot, r, sem):
    lines = src_hbm.shape[1]
    dst = buf.at[slot, pl.ds(pl.multiple_of(r * lines, lines), lines), :]
    return pltpu.make_async_copy(src_hbm.at[row], dst, sem.at[slot])


def _start_rows(src_hbm, idx_ref, n, buf, slot, sem):
    def body(i, c):
        for prio in range(2):
            r = 2 * i + prio
            _row_copy(src_hbm, idx_ref[0, 0, r], buf, slot, r, sem).start(priority=prio)
        return c
    assert n % 2 == 0
    lax.fori_loop(0, n // 2, body, 0, unroll=ROW_DMA_UNROLL // 2)


def _wait_rows(src_hbm, n, buf, slot, sem):
    def body(r, c):
        _row_copy(src_hbm, 0, buf, slot, r, sem).wait()
        return c
    lax.fori_loop(0, n, body, 0, unroll=ROW_DMA_UNROLL)


def _dispatch_kernel(nt_ref, cur_ref, nxt_ref, f_hbm, o_ref, buf, sem, *, tm):
    i = pl.program_id(0)
    nt = nt_ref[0]

    @pl.when(i == 0)
    def _():
        _start_rows(f_hbm, cur_ref, tm, buf, 0, sem)

    @pl.when(i + 1 < nt)
    def _():
        _start_rows(f_hbm, nxt_ref, tm, buf, (i + 1) % 2, sem)

    @pl.when(i < nt)
    def _():
        slot = i % 2
        _wait_rows(f_hbm, tm, buf, slot, sem)
        hi, lo = _unpack_bf16_pairs(_load_row_contiguous(buf.at[slot], 0, tm, f_hbm.shape[1]))
        half = hi.shape[1]
        o_ref[:, :half] = hi.astype(o_ref.dtype)
        o_ref[:, half:] = lo.astype(o_ref.dtype)

    @pl.when(i >= nt)
    def _():
        o_ref[...] = jnp.zeros_like(o_ref)


def _moe_dispatch(f_rows, src_tok, n_used, tm):
    t, lines, lanes = f_rows.shape
    d = 2 * lines * lanes
    n_tiles = src_tok.shape[0] // tm
    src3 = src_tok.reshape(n_tiles, 1, tm)
    smem = functools.partial(pl.BlockSpec, memory_space=pltpu.SMEM)
    grid_spec = pltpu.PrefetchScalarGridSpec(
        num_scalar_prefetch=1,
        grid=(n_tiles,),
        in_specs=[smem((1, 1, tm), lambda i, nt: (i, 0, 0)),
                  smem((1, 1, tm), lambda i, nt: (jnp.minimum(i + 1, n_tiles - 1), 0, 0)),
                  pl.BlockSpec(memory_space=pl.ANY)],
        out_specs=pl.BlockSpec((tm, d), lambda i, nt: (i, 0)),
        scratch_shapes=[pltpu.VMEM((2, tm * lines, lanes), jnp.uint32),
                        pltpu.SemaphoreType.DMA((2,))],
    )
    return pl.pallas_call(
        functools.partial(_dispatch_kernel, tm=tm),
        grid_spec=grid_spec,
        out_shape=jax.ShapeDtypeStruct((n_tiles * tm, d), BF16),
        compiler_params=_params(("arbitrary",), 32),
        name="moe_dispatch",
    )(n_used, src3, src3, f_rows)


TILE_DMA_PRIORITY = 0


def _run_expert_tiles(t0, t1, in_copies, out_copy, prologue, step):
    @pl.when(t1 > t0)
    def _():
        for cp in in_copies(t0, 0):
            cp.start(priority=TILE_DMA_PRIORITY)
        prologue()

        def body(t, c):
            slot = (t - t0) % 2
            for cp in in_copies(t, slot):
                cp.wait()

            @pl.when(t + 1 < t1)
            def _():
                for cp in in_copies(t + 1, 1 - slot):
                    cp.start(priority=TILE_DMA_PRIORITY)

            @pl.when(t - t0 >= 2)
            def _():
                out_copy(t - 2, slot).wait()

            step(slot)
            out_copy(t, slot).start(priority=TILE_DMA_PRIORITY)
            return c
        lax.fori_loop(t0, t1, body, 0)

        @pl.when(t1 - t0 >= 2)
        def _():
            out_copy(t1 - 2, (t1 - 2 - t0) % 2).wait()
        out_copy(t1 - 1, (t1 - 1 - t0) % 2).wait()


def _zero_tail_tiles(first, n_tiles, obuf, out_copy):
    obuf[0] = jnp.zeros(obuf.shape[1:], obuf.dtype)

    def body(t, c):
        out_copy(t, 0).start()
        out_copy(t, 0).wait()
        return c
    lax.fori_loop(first, n_tiles, body, 0)


WEIGHT_DMA_PRIORITY = 1


def _stream_weights(step, n_steps, copy_for_step, wbuf):
    slot = step % 2

    @pl.when(step == 0)
    def _():
        copy_for_step(step, 0).start(priority=WEIGHT_DMA_PRIORITY)

    @pl.when(step + 1 < n_steps)
    def _():
        copy_for_step(step + 1, 1 - slot).start(priority=WEIGHT_DMA_PRIORITY)

    copy_for_step(step, slot).wait()
    return wbuf.at[slot]


def _moe_up_kernel(ts_ref, x_hbm, w_hbm, bg_ref, bl_ref, o_hbm, wbuf, wbf_ref, xbuf, obuf,
                   wsem, xsem, osem, *, tm, n_tiles, layer):
    c = pl.program_id(0)
    e = pl.program_id(1)
    n_experts = pl.num_programs(1)
    half = wbf_ref.shape[1] // 2
    group = V7X_MXU_DIM

    def w_copy(step, slot):
        cols = pl.ds(pl.multiple_of((step // n_experts) * 2 * half, 2 * half), 2 * half)
        return pltpu.make_async_copy(w_hbm.at[layer, step % n_experts, :, cols],
                                     wbuf.at[slot], wsem.at[slot])

    w_ref = _stream_weights(c * n_experts + e, pl.num_programs(0) * n_experts, w_copy, wbuf)

    def in_copies(t, slot):
        return [pltpu.make_async_copy(x_hbm.at[t], xbuf.at[slot], xsem.at[slot])]

    def out_copy(t, slot):
        return pltpu.make_async_copy(obuf.at[slot], o_hbm.at[c, t], osem.at[slot])

    def convert():
        src = lax.broadcasted_iota(jnp.int32, (group, group), 0)
        dst = lax.broadcasted_iota(jnp.int32, (group, group), 1)
        want = jnp.where(dst < group // 2, 2 * dst, 2 * (dst - group // 2) + 1)
        perm = jnp.where(src == want, 1.0, 0.0).astype(BF16)
        for gi in range(wbf_ref.shape[1] // group):
            blk = w_ref[:, gi * group:(gi + 1) * group].astype(BF16)
            split = jnp.dot(blk, perm, preferred_element_type=F32).astype(BF16)
            lo = gi * (group // 2)
            wbf_ref[:, lo:lo + group // 2] = split[:, :group // 2]
            wbf_ref[:, half + lo:half + lo + group // 2] = split[:, group // 2:]

    def step(slot):
        u = jnp.dot(xbuf[slot], wbf_ref[...], preferred_element_type=F32)
        glu = jnp.minimum(u[:, :half] + bg_ref[...], SWIGLU_LIMIT)
        lin = jnp.clip(u[:, half:] + bl_ref[...], -SWIGLU_LIMIT, SWIGLU_LIMIT)
        act = glu * jax.nn.sigmoid(SWIGLU_ALPHA * glu) * (lin + 1.0)
        obuf[slot] = act.astype(obuf.dtype)

    _run_expert_tiles(ts_ref[e], ts_ref[e + 1], in_copies, out_copy, convert, step)

    @pl.when(e == pl.num_programs(1) - 1)
    def _():
        _zero_tail_tiles(ts_ref[e + 1], n_tiles, obuf, out_copy)


def _moe_down_kernel(ts_ref, a_hbm, w_hbm, b_ref, o_hbm, wbuf, wbf_ref, abuf, ybuf, wsem, asem,
                     ysem, *, tm, n_tiles, lines, layer):
    e = pl.program_id(0)
    tf = a_hbm.shape[3]

    def w_copy(step, slot):
        return pltpu.make_async_copy(w_hbm.at[layer, step], wbuf.at[slot], wsem.at[slot])

    w_ref = _stream_weights(e, pl.num_programs(0), w_copy, wbuf)

    def in_copies(t, slot):
        return [pltpu.make_async_copy(a_hbm.at[cc, t],
                                      abuf.at[slot, :, pl.ds(cc * tf, tf)], asem.at[slot])
                for cc in range(a_hbm.shape[0])]

    def out_copy(t, slot):
        return pltpu.make_async_copy(ybuf.at[slot], o_hbm.at[t], ysem.at[slot])

    def convert():
        wbf_ref[...] = w_ref[...].astype(BF16)

    def step(slot):
        y = jnp.dot(abuf[slot], wbf_ref[...], preferred_element_type=F32) + b_ref[...]
        _store_row_contiguous(ybuf.at[slot], _pack_bf16_pairs(y))

    _run_expert_tiles(ts_ref[e], ts_ref[e + 1], in_copies, out_copy, convert, step)

    @pl.when(e == pl.num_programs(0) - 1)
    def _():
        _zero_tail_tiles(ts_ref[e + 1], n_tiles, ybuf, out_copy)


def _moe_experts(x_sorted, tile_start, w1, b1g, b1l, w2, b2, layer, tm):
    p, d = x_sorted.shape
    n_experts, f = w2.shape[1], w2.shape[2]
    n_tiles = p // tm
    tf = _tile(f, 1024)
    n_chunks = f // tf
    lines = _packed_lines(d)
    hbm = pl.BlockSpec(memory_space=pl.ANY)

    up_spec = pltpu.PrefetchScalarGridSpec(
        num_scalar_prefetch=1,
        grid=(n_chunks, n_experts),
        in_specs=[hbm, hbm,
                  pl.BlockSpec((None, None, 1, tf), lambda c, e, ts: (layer, e, 0, c)),
                  pl.BlockSpec((None, None, 1, tf), lambda c, e, ts: (layer, e, 0, c))],
        out_specs=hbm,
        scratch_shapes=[pltpu.VMEM((2, d, 2 * tf), F32), pltpu.VMEM((d, 2 * tf), BF16),
                        pltpu.VMEM((2, tm, d), BF16), pltpu.VMEM((2, tm, tf), BF16),
                        pltpu.SemaphoreType.DMA((2,)), pltpu.SemaphoreType.DMA((2,)),
                        pltpu.SemaphoreType.DMA((2,))],
    )
    act = pl.pallas_call(
        functools.partial(_moe_up_kernel, tm=tm, n_tiles=n_tiles, layer=layer),
        grid_spec=up_spec,
        out_shape=jax.ShapeDtypeStruct((n_chunks, n_tiles, tm, tf), BF16),
        compiler_params=_params(("arbitrary", "arbitrary"), 56),
        name="moe_up",
    )(tile_start, x_sorted.reshape(n_tiles, tm, d), w1, b1g, b1l)

    down_spec = pltpu.PrefetchScalarGridSpec(
        num_scalar_prefetch=1,
        grid=(n_experts,),
        in_specs=[hbm, hbm,
                  pl.BlockSpec((None, None, 1, d), lambda e, ts: (layer, e, 0, 0))],
        out_specs=hbm,
        scratch_shapes=[pltpu.VMEM((2, f, d), F32), pltpu.VMEM((f, d), BF16),
                        pltpu.VMEM((2, tm, f), BF16),
                        pltpu.VMEM((2, tm * lines, V7X_LANES), jnp.uint32),
                        pltpu.SemaphoreType.DMA((2,)), pltpu.SemaphoreType.DMA((2,)),
                        pltpu.SemaphoreType.DMA((2,))],
    )
    y = pl.pallas_call(
        functools.partial(_moe_down_kernel, tm=tm, n_tiles=n_tiles, lines=lines, layer=layer),
        grid_spec=down_spec,
        out_shape=jax.ShapeDtypeStruct((n_tiles, tm * lines, V7X_LANES), jnp.uint32),
        compiler_params=_params(("arbitrary",), 56),
        name="moe_down",
    )(tile_start, act, w2, b2)
    return y.reshape(p * lines, V7X_LANES)


def _combine_kernel(*refs, tc, final_norm):
    if final_norm:
        cur_ref, nxt_ref, y_hbm, h_ref, w_ref, g_ref, fg_ref, o_ref, buf, mix_ref, sem = refs
    else:
        cur_ref, nxt_ref, y_hbm, h_ref, w_ref, g_ref, o_ref, buf, mix_ref, sem = refs
    i = pl.program_id(0)
    n = pl.num_programs(0)
    rows = TOP_K * tc
    lines = y_hbm.shape[1]

    @pl.when(i == 0)
    def _():
        _start_rows(y_hbm, cur_ref, rows, buf, 0, sem)

    @pl.when(i + 1 < n)
    def _():
        _start_rows(y_hbm, nxt_ref, rows, buf, (i + 1) % 2, sem)

    slot = i % 2
    _wait_rows(y_hbm, rows, buf, slot, sem)

    w = w_ref[...]
    blk = tc * lines
    mix_hi = mix_lo = None
    for k in range(TOP_K):
        hi, lo = _unpack_bf16_pairs(buf[slot, pl.ds(k * blk, blk), :])
        wk = w[:, k:k + 1]
        mix_hi = wk * hi if mix_hi is None else mix_hi + wk * hi
        mix_lo = wk * lo if mix_lo is None else mix_lo + wk * lo
    mix_ref[0] = mix_hi
    mix_ref[1] = mix_lo
    mixed = jnp.concatenate([_load_row_contiguous(mix_ref.at[0], 0, tc, lines),
                             _load_row_contiguous(mix_ref.at[1], 0, tc, lines)], axis=1)
    out = h_ref[...] + g_ref[...] * mixed
    if final_norm:
        out = _rms(out, fg_ref[...])
    o_ref[...] = out


def _moe_combine(y_rows, pos, top_w, h, gate, seq, final_g=None):
    t, d = h.shape
    lines = _packed_lines(d)
    tc = _tile(seq, 128)
    n = t // tc
    per_batch = seq // tc
    pos3 = pos.reshape(n, tc, TOP_K).transpose(0, 2, 1).reshape(n, 1, TOP_K * tc)
    w_lines = jnp.repeat(top_w, lines, axis=0)
    smem = functools.partial(pl.BlockSpec, memory_space=pltpu.SMEM)
    final_norm = final_g is not None
    in_specs = [smem((1, 1, TOP_K * tc), lambda i: (i, 0, 0)),
                smem((1, 1, TOP_K * tc), lambda i: (jnp.minimum(i + 1, n - 1), 0, 0)),
                pl.BlockSpec(memory_space=pl.ANY),
                pl.BlockSpec((tc, d), lambda i: (i, 0)),
                pl.BlockSpec((tc * lines, TOP_K), lambda i: (i, 0)),
                pl.BlockSpec((None, 1, d), lambda i: (i // per_batch, 0, 0))]
    args = [pos3, pos3, y_rows, h, w_lines, gate]
    if final_norm:
        in_specs.append(pl.BlockSpec((1, d), lambda i: (0, 0)))
        args.append(final_g.reshape(1, d))
    return pl.pallas_call(
        functools.partial(_combine_kernel, tc=tc, final_norm=final_norm),
        grid=(n,),
        in_specs=in_specs,
        out_specs=pl.BlockSpec((tc, d), lambda i: (i, 0)),
        out_shape=jax.ShapeDtypeStruct((t, d), F32),
        scratch_shapes=[pltpu.VMEM((2, TOP_K * tc * lines, V7X_LANES), jnp.uint32),
                        pltpu.VMEM((2, tc * lines, V7X_LANES), F32),
                        pltpu.SemaphoreType.DMA((2,))],
        compiler_params=_params(("arbitrary",), 32),
        name="moe_combine",
    )(*args)


def _moe_layer(h, layer, seq, norm_g, scale, shift, gate, router_w, router_b,
               moe_w1, moe_b1, moe_w2, moe_b2, final_g=None):
    n_experts = router_w.shape[-1]
    d = h.shape[1]
    lanes = V7X_LANES * pl.cdiv(n_experts, V7X_LANES)
    rw_pad = jnp.pad(router_w[layer], ((0, 0), (0, lanes - n_experts)))
    rb_pad = jnp.pad(router_b[layer], (0, lanes - n_experts),
                     constant_values=NEG_BIG).reshape(1, lanes)
    f_in, top_idx, top_w = _moe_route(h, norm_g, scale, shift, rw_pad, rb_pad, seq)
    tm = 256
    src_tok, pos, tile_start, n_used = _routing_tables(top_idx, n_experts, tm)
    lines = _packed_lines(d)
    x_sorted = _moe_dispatch(f_in.reshape(-1, lines, V7X_LANES), src_tok, n_used, tm)
    f2 = moe_b1.shape[-1]
    b1 = moe_b1.reshape(moe_b1.shape[0], n_experts, f2 // 2, 2)
    b1g = b1[..., 0].reshape(moe_b1.shape[0], n_experts, 1, f2 // 2)
    b1l = b1[..., 1].reshape(moe_b1.shape[0], n_experts, 1, f2 // 2)
    b2 = moe_b2.reshape(moe_b2.shape[0], n_experts, 1, d)
    y_sorted = _moe_experts(x_sorted, tile_start, moe_w1, b1g, b1l, moe_w2, b2, layer, tm)
    return _moe_combine(y_sorted.reshape(-1, lines, V7X_LANES), pos, top_w, h, gate, seq,
                        final_g)


def _mla_weight_layouts(w_in, w_q_up, w_kv_up, heads, q_lora, kv_lora):
    half = ROPE_DIM // 2
    pad = HEAD_DIM - ROPE_DIM
    d = w_in.shape[0]
    rope = w_in[:, q_lora + kv_lora:]
    w_in_p = jnp.concatenate(
        [w_in[:, :q_lora + kv_lora], rope[:, 0::2], rope[:, 1::2], jnp.zeros((d, pad), F32)],
        axis=1).astype(BF16)
    wq = w_q_up.reshape(q_lora, heads, HEAD_DIM + ROPE_DIM)
    wq_rope = wq[:, :, HEAD_DIM:]
    wq_p = jnp.concatenate(
        [wq[:, :, :HEAD_DIM], wq_rope[:, :, 0::2], wq_rope[:, :, 1::2],
         jnp.zeros((q_lora, heads, pad), F32)], axis=2)
    wq_p = wq_p.reshape(q_lora, heads * 2 * HEAD_DIM).astype(BF16)
    wkv = w_kv_up.reshape(kv_lora, heads, 2 * HEAD_DIM)
    wkv_p = jnp.concatenate([wkv[:, :, :HEAD_DIM].reshape(kv_lora, heads * HEAD_DIM),
                             wkv[:, :, HEAD_DIM:].reshape(kv_lora, heads * HEAD_DIM)],
                            axis=1).astype(BF16)
    assert half * 2 == ROPE_DIM and w_in_p.shape[1] == q_lora + kv_lora + HEAD_DIM
    return w_in_p, wq_p, wkv_p


def kernel(x, c, positions, mod_w, mod_b, attn_norm_g, ffn_norm_g, mla_w_in, mla_q_norm_g,
           mla_kv_norm_g, mla_w_q_up, mla_w_kv_up, mla_w_out, shared_norm_g, shared_w_kvf,
           shared_b_f, fox_w_q, fox_w_out, router_w, router_b, moe_w1, moe_b1, moe_w2, moe_b2,
           final_norm_g):
    b, s, d = x.shape
    depth = mod_w.shape[0]
    n_a = mla_w_in.shape[0]
    heads = d // HEAD_DIM
    q_lora = mla_q_norm_g.shape[-1]
    kv_lora = mla_kv_norm_g.shape[-1]
    hd = heads * HEAD_DIM
    t = b * s

    c_pad = jnp.pad(c, ((0, (-b) % V7X_SUBLANES), (0, 0)))
    mod = _modulation(c_pad, mod_w, mod_b)[:, :b, :]
    tables = _rope_tables(positions)

    h = x.reshape(t, d)
    kv_sh = fq = fk = None
    for layer in range(depth):
        sh_a, sc_a, g_a, sh_f, sc_f, g_f = (
            mod[layer, :, i * d:(i + 1) * d].reshape(b, 1, d) for i in range(N_MOD))
        if layer < n_a:
            w_in_p, wq_p, wkv_p = _mla_weight_layouts(
                mla_w_in[layer], mla_w_q_up[layer], mla_w_kv_up[layer], heads, q_lora, kv_lora)
            proj = _norm_matmul(h, attn_norm_g[layer], w_in_p, s, scale=sc_a, shift=sh_a)
            q, k, v = _mla_qkv(proj, mla_q_norm_g[layer], mla_kv_norm_g[layer], wq_p, wkv_p,
                               tables, heads, q_lora, kv_lora,
                               LOG2E * (HEAD_DIM + ROPE_DIM) ** -0.5)
            o = _flash_attention(q.reshape(b, s, -1), k.reshape(b, s, -1),
                                 v.reshape(b, s, -1), heads, HEAD_DIM)
            w_out = mla_w_out[layer].astype(BF16)
        else:
            j = layer - n_a
            q = _norm_matmul(h, attn_norm_g[layer], fox_w_q[j].astype(BF16), s, scale=sc_a,
                             shift=sh_a, out_dtype=BF16,
                             out_scale=LOG2E * HEAD_DIM ** -0.5)
            o = _flash_attention(q.reshape(b, s, hd), kv_sh, kv_sh, heads, HEAD_DIM,
                                 v_block0=heads, fq=fq, fk=fk)
            w_out = fox_w_out[j].astype(BF16)
        h = _out_proj_residual(o.reshape(t, hd), w_out, h, g_a, s)
        last = layer == depth - 1
        h = _moe_layer(h, layer, s, ffn_norm_g[layer], sc_f, sh_f, g_f, router_w, router_b,
                       moe_w1, moe_b1, moe_w2, moe_b2, final_norm_g if last else None)
        if layer == n_a - 1:
            w_f = jnp.pad(shared_w_kvf[:, 2 * hd:], ((0, 0), (0, V7X_LANES - heads)))
            kv_sh, f_logit = _norm_matmul(h, shared_norm_g,
                                          shared_w_kvf[:, :2 * hd].astype(BF16), s,
                                          out_dtype=BF16, side_w=w_f)
            kv_sh = kv_sh.reshape(b, s, 2 * hd)
            b_pad = jnp.pad(shared_b_f, (0, V7X_LANES - heads)).reshape(1, V7X_LANES)
            fq = _forget_cumsum(f_logit.reshape(b, s, V7X_LANES), b_pad, LOG2E)
            fk = fq[:, :, :heads].transpose(0, 2, 1).reshape(b, heads, 1, s)
    return h.reshape(b, s, d)
```

```python
import functools

import jax
import jax.numpy as jnp
from jax import lax
from jax.experimental import pallas as pl
from jax.experimental.pallas import tpu as pltpu

F32 = jnp.float32
BF16 = jnp.bfloat16
HIGHEST = lax.Precision.HIGHEST

RMS_EPS = 1e-6
HEAD_DIM = 128
ROPE_DIM = 64
ROPE_THETA = 10000.0
TOP_K = 4
SWIGLU_ALPHA = 1.702
SWIGLU_LIMIT = 7.0
N_MOD = 6

V7X_LANES = 128
V7X_SUBLANES = 8
V7X_MXU_DIM = 256
V7X_VMEM_BYTES = 64 * 1024 * 1024
MIB = 1024 * 1024

NEG_BIG = -1e30

TOKEN_TILE = 256
WIDE_TOKEN_TILE = 512
COLUMN_CHUNK = 2048
MOD_CHUNK = 1024
FLASH_BLOCK = 512
MOE_ROW_TILE = 256
MOE_UP_CHUNK = 1024
COMBINE_TOKEN_TILE = 128


def _params(semantics, vmem_mib):
    assert vmem_mib * MIB < V7X_VMEM_BYTES
    return pltpu.CompilerParams(dimension_semantics=semantics,
                                vmem_limit_bytes=vmem_mib * MIB)


def _tile(n, pref):
    if n <= pref:
        return n
    t = pref
    while n % t:
        t //= 2
    return t


def _rms(x, g):
    ms = jnp.mean(x * x, axis=-1, keepdims=True)
    return x * lax.rsqrt(ms + RMS_EPS) * g


def _store_row_contiguous(ref, x):
    rows, d = x.shape
    lines = d // V7X_LANES
    for s in range(lines):
        ref[pl.ds(s, rows, stride=lines), :] = x[:, s * V7X_LANES:(s + 1) * V7X_LANES]


def _load_row_contiguous(ref, base, rows, lines):
    return jnp.concatenate(
        [ref[pl.ds(base + s, rows, stride=lines), :] for s in range(lines)], axis=1)


def _pack_bf16_pairs(x):
    half = x.shape[1] // 2
    hi = lax.bitcast_convert_type(x[:, :half].astype(BF16).astype(F32), jnp.uint32)
    lo = lax.bitcast_convert_type(x[:, half:].astype(BF16).astype(F32), jnp.uint32)
    return hi | (lo >> 16)


def _unpack_bf16_pairs(words):
    hi = lax.bitcast_convert_type(words & jnp.uint32(0xFFFF0000), F32)
    lo = lax.bitcast_convert_type(words << 16, F32)
    return hi, lo


def _packed_lines(d):
    return d // (2 * V7X_LANES)


def _mod_kernel(c_ref, w_ref, b_ref, o_ref):
    o_ref[...] = jnp.dot(c_ref[...], w_ref[...], precision=HIGHEST,
                         preferred_element_type=F32) + b_ref[...]


def _modulation(c_pad, mod_w, mod_b):
    n_layers, d, n = mod_w.shape
    rows = c_pad.shape[0]
    tn = _tile(n, MOD_CHUNK)
    return pl.pallas_call(
        _mod_kernel,
        grid=(n_layers, n // tn),
        in_specs=[
            pl.BlockSpec((rows, d), lambda l, j: (0, 0)),
            pl.BlockSpec((None, d, tn), lambda l, j: (l, 0, j)),
            pl.BlockSpec((None, 1, tn), lambda l, j: (l, 0, j)),
        ],
        out_specs=pl.BlockSpec((None, rows, tn), lambda l, j: (l, 0, j)),
        out_shape=jax.ShapeDtypeStruct((n_layers, rows, n), F32),
        compiler_params=_params(("arbitrary", "arbitrary"), 40),
        name="modulation",
    )(c_pad, mod_w, mod_b.reshape(n_layers, 1, n))


def _norm_matmul_kernel(*refs, modulate, side, out_scale):
    refs = list(refs)
    h_ref, g_ref = refs[:2]
    sc_ref, sh_ref = refs[2:4] if modulate else (None, None)
    w_ref = refs[4 if modulate else 2]
    ws_ref = refs[-3] if side else None
    o_ref, os_ref = (refs[-2], refs[-1]) if side else (refs[-1], None)
    y = _rms(h_ref[...], g_ref[...])
    if modulate:
        y = y * (1.0 + sc_ref[...]) + sh_ref[...]
    w = w_ref[...]
    if w.dtype == BF16:
        out = jnp.dot(y.astype(BF16), w, preferred_element_type=F32)
    else:
        out = jnp.dot(y, w, precision=HIGHEST, preferred_element_type=F32)
    if out_scale != 1.0:
        out = out * out_scale
    o_ref[...] = out.astype(o_ref.dtype)
    if side:
        os_ref[...] = jnp.dot(y, ws_ref[...], precision=HIGHEST, preferred_element_type=F32)


def _norm_matmul(h, g, w, seq, *, scale=None, shift=None, out_dtype=F32, out_scale=1.0,
                 side_w=None):
    t, d = h.shape
    n = w.shape[1]
    side = side_w is not None
    tm = _tile(seq, TOKEN_TILE)
    tn = n if side else _tile(n, COLUMN_CHUNK)
    per_batch = seq // tm
    modulate = scale is not None
    in_specs = [pl.BlockSpec((tm, d), lambda j, i: (i, 0)),
                pl.BlockSpec((1, d), lambda j, i: (0, 0))]
    args = [h, g.reshape(1, d)]
    if modulate:
        mod_spec = pl.BlockSpec((None, 1, d), lambda j, i: (i // per_batch, 0, 0))
        in_specs += [mod_spec, mod_spec]
        args += [scale, shift]
    in_specs.append(pl.BlockSpec((d, tn), lambda j, i: (0, j)))
    args.append(w)
    out_specs = pl.BlockSpec((tm, tn), lambda j, i: (i, j))
    out_shape = jax.ShapeDtypeStruct((t, n), out_dtype)
    if side:
        ns = side_w.shape[1]
        in_specs.append(pl.BlockSpec((d, ns), lambda j, i: (0, 0)))
        args.append(side_w)
        out_specs = [out_specs, pl.BlockSpec((tm, ns), lambda j, i: (i, 0))]
        out_shape = [out_shape, jax.ShapeDtypeStruct((t, ns), F32)]
    return pl.pallas_call(
        functools.partial(_norm_matmul_kernel, modulate=modulate, side=side,
                          out_scale=out_scale),
        grid=(n // tn, t // tm),
        in_specs=in_specs,
        out_specs=out_specs,
        out_shape=out_shape,
        compiler_params=_params(("arbitrary", "arbitrary"), 56 if side else 48),
        name="norm_matmul",
    )(*args)


def _rope_table_kernel(pos_ref, freq_ref, c_ref, s1_ref, s2_ref):
    half = ROPE_DIM // 2
    ang = pos_ref[...].astype(F32) * freq_ref[...]
    lane = lax.broadcasted_iota(jnp.int32, ang.shape, 1)
    cos = jnp.cos(ang)
    sin = jnp.sin(ang)
    c_ref[...] = jnp.where(lane < 2 * half, cos, 0.0)
    s1_ref[...] = jnp.where(lane < half, -sin, 0.0)
    s2_ref[...] = jnp.where((lane >= half) & (lane < 2 * half), sin, 0.0)


def _rope_tables(positions):
    t = positions.size
    half = ROPE_DIM // 2
    inv_freq = ROPE_THETA ** (-jnp.arange(0, ROPE_DIM, 2, dtype=F32) / ROPE_DIM)
    freq = jnp.tile(inv_freq, V7X_LANES // half).reshape(1, V7X_LANES)
    tm = _tile(t, WIDE_TOKEN_TILE)
    spec = pl.BlockSpec((tm, V7X_LANES), lambda i: (i, 0))
    shape = jax.ShapeDtypeStruct((t, V7X_LANES), F32)
    return pl.pallas_call(
        _rope_table_kernel,
        grid=(t // tm,),
        in_specs=[pl.BlockSpec((tm, 1), lambda i: (i, 0)),
                  pl.BlockSpec((1, V7X_LANES), lambda i: (0, 0))],
        out_specs=[spec, spec, spec],
        out_shape=[shape, shape, shape],
        compiler_params=_params(("arbitrary",), 16),
        name="rope_tables",
    )(positions.reshape(t, 1), freq)


def _rope_chunk(z, c, s1, s2):
    half = ROPE_DIM // 2
    return (z * c + pltpu.roll(z, V7X_LANES - half, axis=1) * s1
            + pltpu.roll(z, half, axis=1) * s2)


def _mla_q_kernel(x_ref, g_ref, w_ref, c_ref, s1_ref, s2_ref, o_ref, *, heads, scale):
    y = _rms(x_ref[...], g_ref[...]).astype(BF16)
    q = jnp.dot(y, w_ref[...], preferred_element_type=F32) * scale
    c, s1, s2 = c_ref[...], s1_ref[...], s2_ref[...]
    for h in range(heads):
        base = h * 2 * HEAD_DIM
        o_ref[:, base:base + HEAD_DIM] = q[:, base:base + HEAD_DIM].astype(BF16)
        z = q[:, base + HEAD_DIM:base + 2 * HEAD_DIM]
        o_ref[:, base + HEAD_DIM:base + 2 * HEAD_DIM] = _rope_chunk(z, c, s1, s2).astype(BF16)


def _mla_kv_kernel(x_ref, r_ref, g_ref, w_ref, c_ref, s1_ref, s2_ref, k_ref, v_ref, *, heads):
    y = _rms(x_ref[...], g_ref[...]).astype(BF16)
    kv = jnp.dot(y, w_ref[...], preferred_element_type=F32)
    kr = _rope_chunk(r_ref[...], c_ref[...], s1_ref[...], s2_ref[...]).astype(BF16)
    for h in range(heads):
        base = h * 2 * HEAD_DIM
        k_ref[:, base:base + HEAD_DIM] = kv[:, h * HEAD_DIM:(h + 1) * HEAD_DIM].astype(BF16)
        k_ref[:, base + HEAD_DIM:base + 2 * HEAD_DIM] = kr
    v_ref[...] = kv[:, heads * HEAD_DIM:].astype(BF16)


def _mla_qkv(proj, q_norm_g, kv_norm_g, wq, wkv, tables, heads, q_lora, kv_lora, scale):
    t = proj.shape[0]
    tm = _tile(t, TOKEN_TILE)
    dq = heads * 2 * HEAD_DIM
    assert q_lora == kv_lora and q_lora % V7X_LANES == 0
    tab_spec = pl.BlockSpec((tm, V7X_LANES), lambda i: (i, 0))
    rope_block = (q_lora + kv_lora) // V7X_LANES
    q = pl.pallas_call(
        functools.partial(_mla_q_kernel, heads=heads, scale=scale),
        grid=(t // tm,),
        in_specs=[pl.BlockSpec((tm, q_lora), lambda i: (i, 0)),
                  pl.BlockSpec((1, q_lora), lambda i: (0, 0)),
                  pl.BlockSpec((q_lora, dq), lambda i: (0, 0)),
                  tab_spec, tab_spec, tab_spec],
        out_specs=pl.BlockSpec((tm, dq), lambda i: (i, 0)),
        out_shape=jax.ShapeDtypeStruct((t, dq), BF16),
        compiler_params=_params(("arbitrary",), 40),
        name="mla_q",
    )(proj, q_norm_g.reshape(1, q_lora), wq, *tables)
    k, v = pl.pallas_call(
        functools.partial(_mla_kv_kernel, heads=heads),
        grid=(t // tm,),
        in_specs=[pl.BlockSpec((tm, kv_lora), lambda i: (i, 1)),
                  pl.BlockSpec((tm, V7X_LANES), lambda i: (i, rope_block)),
                  pl.BlockSpec((1, kv_lora), lambda i: (0, 0)),
                  pl.BlockSpec((kv_lora, dq), lambda i: (0, 0)),
                  tab_spec, tab_spec, tab_spec],
        out_specs=[pl.BlockSpec((tm, dq), lambda i: (i, 0)),
                   pl.BlockSpec((tm, heads * HEAD_DIM), lambda i: (i, 0))],
        out_shape=[jax.ShapeDtypeStruct((t, dq), BF16),
                   jax.ShapeDtypeStruct((t, heads * HEAD_DIM), BF16)],
        compiler_params=_params(("arbitrary",), 40),
        name="mla_kv",
    )(proj, proj, kv_norm_g.reshape(1, kv_lora), wkv, *tables)
    return q, k, v


FLASH_HEADS_PER_STEP = 4
LOG2E = 1.4426950408889634


def _flash_kernel(*refs, tq, dk, dv, hp, decay):
    if decay:
        q_ref, k_ref, v_ref, fq_ref, fk_ref, o_ref = refs
    else:
        q_ref, k_ref, v_ref, o_ref = refs
    qi = pl.program_id(2)
    qs = [q_ref[:, a * dk:(a + 1) * dk] for a in range(hp)]
    if decay:
        fq_all = fq_ref[...]
        lane = lax.broadcasted_iota(jnp.int32, fq_all.shape, 1)
        fqs = [jnp.sum(jnp.where(lane == pl.program_id(1) * hp + a, fq_all, 0.0), axis=-1,
                       keepdims=True) for a in range(hp)]

    def block(j, carry, masked):
        start = pl.multiple_of(j * tq, tq)
        out = []
        for a in range(hp):
            m, l, acc = carry[a]
            k = k_ref[pl.ds(start, tq), a * dk:(a + 1) * dk]
            v = v_ref[pl.ds(start, tq), a * dv:(a + 1) * dv]
            s = lax.dot_general(qs[a], k, (((1,), (1,)), ((), ())),
                                preferred_element_type=F32)
            if decay:
                s = s + (fqs[a] - fk_ref[a, :, pl.ds(start, tq)])
            if masked:
                row = lax.broadcasted_iota(jnp.int32, s.shape, 0)
                col = lax.broadcasted_iota(jnp.int32, s.shape, 1)
                s = jnp.where(col <= row, s, -jnp.inf)
            m_new = jnp.maximum(m, jnp.max(s, axis=-1, keepdims=True))
            p = jnp.exp2(s - m_new)
            alpha = jnp.exp2(m - m_new)
            l = alpha * l + jnp.sum(p, axis=-1, keepdims=True)
            acc = alpha * acc + jnp.dot(p.astype(BF16), v, preferred_element_type=F32)
            out.append((m_new, l, acc))
        return tuple(out)

    init = tuple((jnp.full((tq, 1), -jnp.inf, F32), jnp.zeros((tq, 1), F32),
                  jnp.zeros((tq, dv), F32)) for _ in range(hp))
    carry = lax.fori_loop(0, qi, lambda j, c: block(j, c, False), init)
    final = block(qi, carry, True)
    for a in range(hp):
        _, l, acc = final[a]
        o_ref[:, a * dv:(a + 1) * dv] = (acc / l).astype(o_ref.dtype)


def _flash_attention(q, k, v, heads, dv, v_block0=0, fq=None, fk=None):
    b, s, _ = q.shape
    dk = q.shape[-1] // heads
    tq = _tile(s, FLASH_BLOCK)
    hp = min(FLASH_HEADS_PER_STEP, heads)
    assert heads % hp == 0 and v_block0 % hp == 0
    vb0 = v_block0 // hp
    decay = fq is not None
    in_specs = [pl.BlockSpec((None, tq, hp * dk), lambda bi, h, i: (bi, i, h)),
                pl.BlockSpec((None, s, hp * dk), lambda bi, h, i: (bi, 0, h)),
                pl.BlockSpec((None, s, hp * dv), lambda bi, h, i: (bi, 0, vb0 + h))]
    args = [q, k, v]
    if decay:
        in_specs += [pl.BlockSpec((None, tq, V7X_LANES), lambda bi, h, i: (bi, i, 0)),
                     pl.BlockSpec((None, hp, 1, s), lambda bi, h, i: (bi, h, 0, 0))]
        args += [fq, fk]
    return pl.pallas_call(
        functools.partial(_flash_kernel, tq=tq, dk=dk, dv=dv, hp=hp, decay=decay),
        grid=(b, heads // hp, s // tq),
        in_specs=in_specs,
        out_specs=pl.BlockSpec((None, tq, hp * dv), lambda bi, h, i: (bi, i, h)),
        out_shape=jax.ShapeDtypeStruct((b, s, heads * dv), BF16),
        compiler_params=_params(("arbitrary", "arbitrary", "arbitrary"), 40),
        name="flash_attention",
    )(*args)


def _out_proj_kernel(o_ref, w_ref, h_ref, g_ref, out_ref):
    a = jnp.dot(o_ref[...], w_ref[...], preferred_element_type=F32)
    out_ref[...] = h_ref[...] + g_ref[...] * a


def _out_proj_residual(o, w, h, gate, seq):
    t, dk = o.shape
    d = w.shape[1]
    tm = _tile(seq, TOKEN_TILE)
    per_batch = seq // tm
    return pl.pallas_call(
        _out_proj_kernel,
        grid=(t // tm,),
        in_specs=[pl.BlockSpec((tm, dk), lambda i: (i, 0)),
                  pl.BlockSpec((dk, d), lambda i: (0, 0)),
                  pl.BlockSpec((tm, d), lambda i: (i, 0)),
                  pl.BlockSpec((None, 1, d), lambda i: (i // per_batch, 0, 0))],
        out_specs=pl.BlockSpec((tm, d), lambda i: (i, 0)),
        out_shape=jax.ShapeDtypeStruct((t, d), F32),
        compiler_params=_params(("arbitrary",), 40),
        name="out_proj_residual",
    )(o, w, h, gate)


def _forget_cumsum_kernel(f_ref, b_ref, o_ref, carry_ref, *, out_scale):
    @pl.when(pl.program_id(1) == 0)
    def _():
        carry_ref[...] = jnp.zeros_like(carry_ref)

    x = f_ref[...] + b_ref[...]
    log_f = jnp.minimum(x, 0.0) - jnp.log1p(jnp.exp(-jnp.abs(x)))
    ts = x.shape[0]
    row = lax.broadcasted_iota(jnp.int32, (ts, ts), 0)
    col = lax.broadcasted_iota(jnp.int32, (ts, ts), 1)
    tri = jnp.where(col <= row, 1.0, 0.0).astype(F32)
    c = jnp.dot(tri, log_f, precision=HIGHEST, preferred_element_type=F32) + carry_ref[...]
    o_ref[...] = c * out_scale
    carry_ref[...] = c[ts - 1:ts, :]


def _forget_cumsum(f_logit, b_pad, out_scale):
    b, s, lanes = f_logit.shape
    ts = _tile(s, WIDE_TOKEN_TILE)
    return pl.pallas_call(
        functools.partial(_forget_cumsum_kernel, out_scale=out_scale),
        grid=(b, s // ts),
        in_specs=[pl.BlockSpec((None, ts, lanes), lambda bi, i: (bi, i, 0)),
                  pl.BlockSpec((1, lanes), lambda bi, i: (0, 0))],
        out_specs=pl.BlockSpec((None, ts, lanes), lambda bi, i: (bi, i, 0)),
        out_shape=jax.ShapeDtypeStruct((b, s, lanes), F32),
        scratch_shapes=[pltpu.VMEM((1, lanes), F32)],
        compiler_params=_params(("arbitrary", "arbitrary"), 16),
        name="forget_cumsum",
    )(f_logit, b_pad)


def _moe_route_kernel(h_ref, g_ref, sc_ref, sh_ref, rw_ref, rb_ref, f_ref, idx_ref, w_ref):
    y = _rms(h_ref[...], g_ref[...]) * (1.0 + sc_ref[...]) + sh_ref[...]
    _store_row_contiguous(f_ref, _pack_bf16_pairs(y))
    logits = jnp.dot(y, rw_ref[...], precision=HIGHEST,
                     preferred_element_type=F32) + rb_ref[...]
    lane = lax.broadcasted_iota(jnp.int32, logits.shape, 1).astype(F32)
    vals, idxs = [], []
    for _ in range(TOP_K):
        m = jnp.max(logits, axis=-1, keepdims=True)
        ix = jnp.min(jnp.where(logits == m, lane, float(logits.shape[1])), axis=-1,
                     keepdims=True)
        vals.append(m)
        idxs.append(ix)
        logits = jnp.where(lane == ix, -jnp.inf, logits)
    exps = [jnp.exp(v - vals[0]) for v in vals]
    den = exps[0]
    for e in exps[1:]:
        den = den + e
    slot = lax.broadcasted_iota(jnp.int32, idx_ref.shape, 1)
    idx_out = jnp.zeros(idx_ref.shape, jnp.int32)
    w_out = jnp.zeros(w_ref.shape, F32)
    for k in range(TOP_K):
        idx_out = jnp.where(slot == k, idxs[k].astype(jnp.int32), idx_out)
        w_out = jnp.where(slot == k, exps[k] / den, w_out)
    idx_ref[...] = idx_out
    w_ref[...] = w_out


def _moe_route(h, g, scale, shift, rw_pad, rb_pad, seq):
    t, d = h.shape
    tm = _tile(seq, TOKEN_TILE)
    per_batch = seq // tm
    lanes = rw_pad.shape[1]
    lines = _packed_lines(d)
    mod_spec = pl.BlockSpec((None, 1, d), lambda i: (i // per_batch, 0, 0))
    return pl.pallas_call(
        _moe_route_kernel,
        grid=(t // tm,),
        in_specs=[pl.BlockSpec((tm, d), lambda i: (i, 0)),
                  pl.BlockSpec((1, d), lambda i: (0, 0)),
                  mod_spec, mod_spec,
                  pl.BlockSpec((d, lanes), lambda i: (0, 0)),
                  pl.BlockSpec((1, lanes), lambda i: (0, 0))],
        out_specs=[pl.BlockSpec((tm * lines, V7X_LANES), lambda i: (i, 0)),
                   pl.BlockSpec((tm, TOP_K), lambda i: (i, 0)),
                   pl.BlockSpec((tm, TOP_K), lambda i: (i, 0))],
        out_shape=[jax.ShapeDtypeStruct((t * lines, V7X_LANES), jnp.uint32),
                   jax.ShapeDtypeStruct((t, TOP_K), jnp.int32),
                   jax.ShapeDtypeStruct((t, TOP_K), F32)],
        compiler_params=_params(("arbitrary",), 32),
        name="moe_route",
    )(h, g.reshape(1, d), scale, shift, rw_pad, rb_pad)


def _routing_tables(top_idx, n_experts, tm):
    t, k = top_idx.shape
    n_tiles = (t * k) // tm + n_experts
    e_flat = top_idx.reshape(-1)
    onehot = (e_flat[:, None] == jnp.arange(n_experts, dtype=jnp.int32)[None, :]).astype(jnp.int32)
    counts = jnp.sum(onehot, axis=0)
    count_start = jnp.cumsum(counts) - counts
    order = jnp.argsort(e_flat, stable=True).astype(jnp.int32)
    sorted_place = jnp.argsort(order).astype(jnp.int32)
    tiles_per = (counts + tm - 1) // tm
    tile_end = jnp.cumsum(tiles_per)
    tile_start = jnp.concatenate([jnp.zeros((1,), jnp.int32), tile_end]).astype(jnp.int32)
    row_start = tile_start[:-1] * tm
    pos = (row_start - count_start)[e_flat] + sorted_place
    n_used = tile_end[-1]
    tile_ids = jnp.arange(n_tiles, dtype=jnp.int32)
    tile_expert = jnp.sum((tile_ids[:, None] >= tile_end[None, :]).astype(jnp.int32), axis=1)
    tile_expert = jnp.minimum(tile_expert, n_experts - 1)
    in_tile = jnp.arange(tm, dtype=jnp.int32)[None, :]
    r = (tile_ids * tm - row_start[tile_expert])[:, None] + in_tile
    valid = r < counts[tile_expert][:, None]
    src_flat = order[jnp.clip(count_start[tile_expert][:, None] + r, 0, t * k - 1)]
    src_tok = jnp.where(valid, src_flat // k, 0).reshape(n_tiles * tm)
    return (src_tok.astype(jnp.int32), pos.reshape(t, k), tile_start,
            n_used.reshape(1).astype(jnp.int32))


ROW_DMA_UNROLL = 8


def _row_copy(src_hbm, row, buf, slot, r, sem):
    lines = src_hbm.shape[1]
    dst = buf.at[slot, pl.ds(pl.multiple_of(r * lines, lines), lines), :]
    return pltpu.make_async_copy(src_hbm.at[row], dst, sem.at[slot])


def _start_rows(src_hbm, idx_ref, n, buf, slot, sem):
    def body(i, c):
        for prio in range(2):
            r = 2 * i + prio
            _row_copy(src_hbm, idx_ref[0, 0, r], buf, slot, r, sem).start(priority=prio)
        return c
    assert n % 2 == 0
    lax.fori_loop(0, n // 2, body, 0, unroll=ROW_DMA_UNROLL // 2)


def _wait_rows(src_hbm, n, buf, slot, sem):
    def body(r, c):
        _row_copy(src_hbm, 0, buf, slot, r, sem).wait()
        return c
    lax.fori_loop(0, n, body, 0, unroll=ROW_DMA_UNROLL)


def _dispatch_kernel(nt_ref, cur_ref, nxt_ref, f_hbm, o_ref, buf, sem, *, tm):
    i = pl.program_id(0)
    nt = nt_ref[0]

    @pl.when(i == 0)
    def _():
        _start_rows(f_hbm, cur_ref, tm, buf, 0, sem)

    @pl.when(i + 1 < nt)
    def _():
        _start_rows(f_hbm, nxt_ref, tm, buf, (i + 1) % 2, sem)

    @pl.when(i < nt)
    def _():
        slot = i % 2
        _wait_rows(f_hbm, tm, buf, slot, sem)
        hi, lo = _unpack_bf16_pairs(_load_row_contiguous(buf.at[slot], 0, tm, f_hbm.shape[1]))
        half = hi.shape[1]
        o_ref[:, :half] = hi.astype(o_ref.dtype)
        o_ref[:, half:] = lo.astype(o_ref.dtype)

    @pl.when(i >= nt)
    def _():
        o_ref[...] = jnp.zeros_like(o_ref)


def _moe_dispatch(f_rows, src_tok, n_used, tm):
    t, lines, lanes = f_rows.shape
    d = 2 * lines * lanes
    n_tiles = src_tok.shape[0] // tm
    src3 = src_tok.reshape(n_tiles, 1, tm)
    smem = functools.partial(pl.BlockSpec, memory_space=pltpu.SMEM)
    grid_spec = pltpu.PrefetchScalarGridSpec(
        num_scalar_prefetch=1,
        grid=(n_tiles,),
        in_specs=[smem((1, 1, tm), lambda i, nt: (i, 0, 0)),
                  smem((1, 1, tm), lambda i, nt: (jnp.minimum(i + 1, n_tiles - 1), 0, 0)),
                  pl.BlockSpec(memory_space=pl.ANY)],
        out_specs=pl.BlockSpec((tm, d), lambda i, nt: (i, 0)),
        scratch_shapes=[pltpu.VMEM((2, tm * lines, lanes), jnp.uint32),
                        pltpu.SemaphoreType.DMA((2,))],
    )
    return pl.pallas_call(
        functools.partial(_dispatch_kernel, tm=tm),
        grid_spec=grid_spec,
        out_shape=jax.ShapeDtypeStruct((n_tiles * tm, d), BF16),
        compiler_params=_params(("arbitrary",), 32),
        name="moe_dispatch",
    )(n_used, src3, src3, f_rows)


TILE_DMA_PRIORITY = 0


def _run_expert_tiles(t0, t1, in_copies, out_copy, prologue, step):
    @pl.when(t1 > t0)
    def _():
        for cp in in_copies(t0, 0):
            cp.start(priority=TILE_DMA_PRIORITY)
        prologue()

        def body(t, c):
            slot = (t - t0) % 2
            for cp in in_copies(t, slot):
                cp.wait()

            @pl.when(t + 1 < t1)
            def _():
                for cp in in_copies(t + 1, 1 - slot):
                    cp.start(priority=TILE_DMA_PRIORITY)

            @pl.when(t - t0 >= 2)
            def _():
                out_copy(t - 2, slot).wait()

            step(slot)
            out_copy(t, slot).start(priority=TILE_DMA_PRIORITY)
            return c
        lax.fori_loop(t0, t1, body, 0)

        @pl.when(t1 - t0 >= 2)
        def _():
            out_copy(t1 - 2, (t1 - 2 - t0) % 2).wait()
        out_copy(t1 - 1, (t1 - 1 - t0) % 2).wait()


def _zero_tail_tiles(first, n_tiles, obuf, out_copy):
    obuf[0] = jnp.zeros(obuf.shape[1:], obuf.dtype)

    def body(t, c):
        out_copy(t, 0).start()
        out_copy(t, 0).wait()
        return c
    lax.fori_loop(first, n_tiles, body, 0)


WEIGHT_DMA_PRIORITY = 1


def _stream_weights(step, n_steps, copy_for_step, wbuf):
    slot = step % 2

    @pl.when(step == 0)
    def _():
        copy_for_step(step, 0).start(priority=WEIGHT_DMA_PRIORITY)

    @pl.when(step + 1 < n_steps)
    def _():
        copy_for_step(step + 1, 1 - slot).start(priority=WEIGHT_DMA_PRIORITY)

    copy_for_step(step, slot).wait()
    return wbuf.at[slot]


def _moe_up_kernel(ts_ref, x_hbm, w_hbm, bg_ref, bl_ref, o_hbm, wbuf, wbf_ref, xbuf, obuf,
                   wsem, xsem, osem, *, tm, n_tiles, layer):
    c = pl.program_id(0)
    e = pl.program_id(1)
    n_experts = pl.num_programs(1)
    half = wbf_ref.shape[1] // 2
    group = V7X_MXU_DIM

    def w_copy(step, slot):
        cols = pl.ds(pl.multiple_of((step // n_experts) * 2 * half, 2 * half), 2 * half)
        return pltpu.make_async_copy(w_hbm.at[layer, step % n_experts, :, cols],
                                     wbuf.at[slot], wsem.at[slot])

    w_ref = _stream_weights(c * n_experts + e, pl.num_programs(0) * n_experts, w_copy, wbuf)

    def in_copies(t, slot):
        return [pltpu.make_async_copy(x_hbm.at[t], xbuf.at[slot], xsem.at[slot])]

    def out_copy(t, slot):
        return pltpu.make_async_copy(obuf.at[slot], o_hbm.at[c, t], osem.at[slot])

    def convert():
        src = lax.broadcasted_iota(jnp.int32, (group, group), 0)
        dst = lax.broadcasted_iota(jnp.int32, (group, group), 1)
        want = jnp.where(dst < group // 2, 2 * dst, 2 * (dst - group // 2) + 1)
        perm = jnp.where(src == want, 1.0, 0.0).astype(BF16)
        for gi in range(wbf_ref.shape[1] // group):
            blk = w_ref[:, gi * group:(gi + 1) * group].astype(BF16)
            split = jnp.dot(blk, perm, preferred_element_type=F32).astype(BF16)
            lo = gi * (group // 2)
            wbf_ref[:, lo:lo + group // 2] = split[:, :group // 2]
            wbf_ref[:, half + lo:half + lo + group // 2] = split[:, group // 2:]

    def step(slot):
        u = jnp.dot(xbuf[slot], wbf_ref[...], preferred_element_type=F32)
        glu = jnp.minimum(u[:, :half] + bg_ref[...], SWIGLU_LIMIT)
        lin = jnp.clip(u[:, half:] + bl_ref[...], -SWIGLU_LIMIT, SWIGLU_LIMIT)
        act = glu * jax.nn.sigmoid(SWIGLU_ALPHA * glu) * (lin + 1.0)
        obuf[slot] = act.astype(obuf.dtype)

    _run_expert_tiles(ts_ref[e], ts_ref[e + 1], in_copies, out_copy, convert, step)

    @pl.when(e == pl.num_programs(1) - 1)
    def _():
        _zero_tail_tiles(ts_ref[e + 1], n_tiles, obuf, out_copy)


def _moe_down_kernel(ts_ref, a_hbm, w_hbm, b_ref, o_hbm, wbuf, wbf_ref, abuf, ybuf, wsem, asem,
                     ysem, *, tm, n_tiles, lines, layer):
    e = pl.program_id(0)
    tf = a_hbm.shape[3]

    def w_copy(step, slot):
        return pltpu.make_async_copy(w_hbm.at[layer, step], wbuf.at[slot], wsem.at[slot])

    w_ref = _stream_weights(e, pl.num_programs(0), w_copy, wbuf)

    def in_copies(t, slot):
        return [pltpu.make_async_copy(a_hbm.at[cc, t],
                                      abuf.at[slot, :, pl.ds(cc * tf, tf)], asem.at[slot])
                for cc in range(a_hbm.shape[0])]

    def out_copy(t, slot):
        return pltpu.make_async_copy(ybuf.at[slot], o_hbm.at[t], ysem.at[slot])

    def convert():
        wbf_ref[...] = w_ref[...].astype(BF16)

    def step(slot):
        y = jnp.dot(abuf[slot], wbf_ref[...], preferred_element_type=F32) + b_ref[...]
        _store_row_contiguous(ybuf.at[slot], _pack_bf16_pairs(y))

    _run_expert_tiles(ts_ref[e], ts_ref[e + 1], in_copies, out_copy, convert, step)

    @pl.when(e == pl.num_programs(0) - 1)
    def _():
        _zero_tail_tiles(ts_ref[e + 1], n_tiles, ybuf, out_copy)


def _moe_experts(x_sorted, tile_start, w1, b1g, b1l, w2, b2, layer, tm):
    p, d = x_sorted.shape
    n_experts, f = w2.shape[1], w2.shape[2]
    n_tiles = p // tm
    tf = _tile(f, MOE_UP_CHUNK)
    n_chunks = f // tf
    lines = _packed_lines(d)
    hbm = pl.BlockSpec(memory_space=pl.ANY)

    up_spec = pltpu.PrefetchScalarGridSpec(
        num_scalar_prefetch=1,
        grid=(n_chunks, n_experts),
        in_specs=[hbm, hbm,
                  pl.BlockSpec((None, None, 1, tf), lambda c, e, ts: (layer, e, 0, c)),
                  pl.BlockSpec((None, None, 1, tf), lambda c, e, ts: (layer, e, 0, c))],
        out_specs=hbm,
        scratch_shapes=[pltpu.VMEM((2, d, 2 * tf), F32), pltpu.VMEM((d, 2 * tf), BF16),
                        pltpu.VMEM((2, tm, d), BF16), pltpu.VMEM((2, tm, tf), BF16),
                        pltpu.SemaphoreType.DMA((2,)), pltpu.SemaphoreType.DMA((2,)),
                        pltpu.SemaphoreType.DMA((2,))],
    )
    act = pl.pallas_call(
        functools.partial(_moe_up_kernel, tm=tm, n_tiles=n_tiles, layer=layer),
        grid_spec=up_spec,
        out_shape=jax.ShapeDtypeStruct((n_chunks, n_tiles, tm, tf), BF16),
        compiler_params=_params(("arbitrary", "arbitrary"), 56),
        name="moe_up",
    )(tile_start, x_sorted.reshape(n_tiles, tm, d), w1, b1g, b1l)

    down_spec = pltpu.PrefetchScalarGridSpec(
        num_scalar_prefetch=1,
        grid=(n_experts,),
        in_specs=[hbm, hbm,
                  pl.BlockSpec((None, None, 1, d), lambda e, ts: (layer, e, 0, 0))],
        out_specs=hbm,
        scratch_shapes=[pltpu.VMEM((2, f, d), F32), pltpu.VMEM((f, d), BF16),
                        pltpu.VMEM((2, tm, f), BF16),
                        pltpu.VMEM((2, tm * lines, V7X_LANES), jnp.uint32),
                        pltpu.SemaphoreType.DMA((2,)), pltpu.SemaphoreType.DMA((2,)),
                        pltpu.SemaphoreType.DMA((2,))],
    )
    y = pl.pallas_call(
        functools.partial(_moe_down_kernel, tm=tm, n_tiles=n_tiles, lines=lines, layer=layer),
        grid_spec=down_spec,
        out_shape=jax.ShapeDtypeStruct((n_tiles, tm * lines, V7X_LANES), jnp.uint32),
        compiler_params=_params(("arbitrary",), 56),
        name="moe_down",
    )(tile_start, act, w2, b2)
    return y.reshape(p * lines, V7X_LANES)


def _combine_kernel(*refs, tc, final_norm):
    if final_norm:
        cur_ref, nxt_ref, y_hbm, h_ref, w_ref, g_ref, fg_ref, o_ref, buf, mix_ref, sem = refs
    else:
        cur_ref, nxt_ref, y_hbm, h_ref, w_ref, g_ref, o_ref, buf, mix_ref, sem = refs
    i = pl.program_id(0)
    n = pl.num_programs(0)
    rows = TOP_K * tc
    lines = y_hbm.shape[1]

    @pl.when(i == 0)
    def _():
        _start_rows(y_hbm, cur_ref, rows, buf, 0, sem)

    @pl.when(i + 1 < n)
    def _():
        _start_rows(y_hbm, nxt_ref, rows, buf, (i + 1) % 2, sem)

    slot = i % 2
    _wait_rows(y_hbm, rows, buf, slot, sem)

    w = w_ref[...]
    blk = tc * lines
    mix_hi = mix_lo = None
    for k in range(TOP_K):
        hi, lo = _unpack_bf16_pairs(buf[slot, pl.ds(k * blk, blk), :])
        wk = w[:, k:k + 1]
        mix_hi = wk * hi if mix_hi is None else mix_hi + wk * hi
        mix_lo = wk * lo if mix_lo is None else mix_lo + wk * lo
    mix_ref[0] = mix_hi
    mix_ref[1] = mix_lo
    mixed = jnp.concatenate([_load_row_contiguous(mix_ref.at[0], 0, tc, lines),
                             _load_row_contiguous(mix_ref.at[1], 0, tc, lines)], axis=1)
    out = h_ref[...] + g_ref[...] * mixed
    if final_norm:
        out = _rms(out, fg_ref[...])
    o_ref[...] = out


def _moe_combine(y_rows, pos, top_w, h, gate, seq, final_g=None):
    t, d = h.shape
    lines = _packed_lines(d)
    tc = _tile(seq, COMBINE_TOKEN_TILE)
    n = t // tc
    per_batch = seq // tc
    pos3 = pos.reshape(n, tc, TOP_K).transpose(0, 2, 1).reshape(n, 1, TOP_K * tc)
    w_lines = jnp.repeat(top_w, lines, axis=0)
    smem = functools.partial(pl.BlockSpec, memory_space=pltpu.SMEM)
    final_norm = final_g is not None
    in_specs = [smem((1, 1, TOP_K * tc), lambda i: (i, 0, 0)),
                smem((1, 1, TOP_K * tc), lambda i: (jnp.minimum(i + 1, n - 1), 0, 0)),
                pl.BlockSpec(memory_space=pl.ANY),
                pl.BlockSpec((tc, d), lambda i: (i, 0)),
                pl.BlockSpec((tc * lines, TOP_K), lambda i: (i, 0)),
                pl.BlockSpec((None, 1, d), lambda i: (i // per_batch, 0, 0))]
    args = [pos3, pos3, y_rows, h, w_lines, gate]
    if final_norm:
        in_specs.append(pl.BlockSpec((1, d), lambda i: (0, 0)))
        args.append(final_g.reshape(1, d))
    return pl.pallas_call(
        functools.partial(_combine_kernel, tc=tc, final_norm=final_norm),
        grid=(n,),
        in_specs=in_specs,
        out_specs=pl.BlockSpec((tc, d), lambda i: (i, 0)),
        out_shape=jax.ShapeDtypeStruct((t, d), F32),
        scratch_shapes=[pltpu.VMEM((2, TOP_K * tc * lines, V7X_LANES), jnp.uint32),
                        pltpu.VMEM((2, tc * lines, V7X_LANES), F32),
                        pltpu.SemaphoreType.DMA((2,))],
        compiler_params=_params(("arbitrary",), 32),
        name="moe_combine",
    )(*args)


def _moe_layer(h, layer, seq, norm_g, scale, shift, gate, router_w, router_b,
               moe_w1, moe_b1, moe_w2, moe_b2, final_g=None):
    n_experts = router_w.shape[-1]
    d = h.shape[1]
    lanes = V7X_LANES * pl.cdiv(n_experts, V7X_LANES)
    rw_pad = jnp.pad(router_w[layer], ((0, 0), (0, lanes - n_experts)))
    rb_pad = jnp.pad(router_b[layer], (0, lanes - n_experts),
                     constant_values=NEG_BIG).reshape(1, lanes)
    f_in, top_idx, top_w = _moe_route(h, norm_g, scale, shift, rw_pad, rb_pad, seq)
    tm = MOE_ROW_TILE
    src_tok, pos, tile_start, n_used = _routing_tables(top_idx, n_experts, tm)
    lines = _packed_lines(d)
    x_sorted = _moe_dispatch(f_in.reshape(-1, lines, V7X_LANES), src_tok, n_used, tm)
    f2 = moe_b1.shape[-1]
    b1 = moe_b1.reshape(moe_b1.shape[0], n_experts, f2 // 2, 2)
    b1g = b1[..., 0].reshape(moe_b1.shape[0], n_experts, 1, f2 // 2)
    b1l = b1[..., 1].reshape(moe_b1.shape[0], n_experts, 1, f2 // 2)
    b2 = moe_b2.reshape(moe_b2.shape[0], n_experts, 1, d)
    y_sorted = _moe_experts(x_sorted, tile_start, moe_w1, b1g, b1l, moe_w2, b2, layer, tm)
    return _moe_combine(y_sorted.reshape(-1, lines, V7X_LANES), pos, top_w, h, gate, seq,
                        final_g)


def _mla_weight_layouts(w_in, w_q_up, w_kv_up, heads, q_lora, kv_lora):
    half = ROPE_DIM // 2
    pad = HEAD_DIM - ROPE_DIM
    d = w_in.shape[0]
    rope = w_in[:, q_lora + kv_lora:]
    w_in_p = jnp.concatenate(
        [w_in[:, :q_lora + kv_lora], rope[:, 0::2], rope[:, 1::2], jnp.zeros((d, pad), F32)],
        axis=1).astype(BF16)
    wq = w_q_up.reshape(q_lora, heads, HEAD_DIM + ROPE_DIM)
    wq_rope = wq[:, :, HEAD_DIM:]
    wq_p = jnp.concatenate(
        [wq[:, :, :HEAD_DIM], wq_rope[:, :, 0::2], wq_rope[:, :, 1::2],
         jnp.zeros((q_lora, heads, pad), F32)], axis=2)
    wq_p = wq_p.reshape(q_lora, heads * 2 * HEAD_DIM).astype(BF16)
    wkv = w_kv_up.reshape(kv_lora, heads, 2 * HEAD_DIM)
    wkv_p = jnp.concatenate([wkv[:, :, :HEAD_DIM].reshape(kv_lora, heads * HEAD_DIM),
                             wkv[:, :, HEAD_DIM:].reshape(kv_lora, heads * HEAD_DIM)],
                            axis=1).astype(BF16)
    assert half * 2 == ROPE_DIM and w_in_p.shape[1] == q_lora + kv_lora + HEAD_DIM
    return w_in_p, wq_p, wkv_p


def kernel(x, c, positions, mod_w, mod_b, attn_norm_g, ffn_norm_g, mla_w_in, mla_q_norm_g,
           mla_kv_norm_g, mla_w_q_up, mla_w_kv_up, mla_w_out, shared_norm_g, shared_w_kvf,
           shared_b_f, fox_w_q, fox_w_out, router_w, router_b, moe_w1, moe_b1, moe_w2, moe_b2,
           final_norm_g):
    b, s, d = x.shape
    depth = mod_w.shape[0]
    n_a = mla_w_in.shape[0]
    heads = d // HEAD_DIM
    q_lora = mla_q_norm_g.shape[-1]
    kv_lora = mla_kv_norm_g.shape[-1]
    hd = heads * HEAD_DIM
    t = b * s

    c_pad = jnp.pad(c, ((0, (-b) % V7X_SUBLANES), (0, 0)))
    mod = _modulation(c_pad, mod_w, mod_b)[:, :b, :]
    tables = _rope_tables(positions)

    h = x.reshape(t, d)
    kv_sh = fq = fk = None
    for layer in range(depth):
        sh_a, sc_a, g_a, sh_f, sc_f, g_f = (
            mod[layer, :, i * d:(i + 1) * d].reshape(b, 1, d) for i in range(N_MOD))
        if layer < n_a:
            w_in_p, wq_p, wkv_p = _mla_weight_layouts(
                mla_w_in[layer], mla_w_q_up[layer], mla_w_kv_up[layer], heads, q_lora, kv_lora)
            proj = _norm_matmul(h, attn_norm_g[layer], w_in_p, s, scale=sc_a, shift=sh_a)
            q, k, v = _mla_qkv(proj, mla_q_norm_g[layer], mla_kv_norm_g[layer], wq_p, wkv_p,
                               tables, heads, q_lora, kv_lora,
                               LOG2E * (HEAD_DIM + ROPE_DIM) ** -0.5)
            o = _flash_attention(q.reshape(b, s, -1), k.reshape(b, s, -1),
                                 v.reshape(b, s, -1), heads, HEAD_DIM)
            w_out = mla_w_out[layer].astype(BF16)
        else:
            j = layer - n_a
            q = _norm_matmul(h, attn_norm_g[layer], fox_w_q[j].astype(BF16), s, scale=sc_a,
                             shift=sh_a, out_dtype=BF16,
                             out_scale=LOG2E * HEAD_DIM ** -0.5)
            o = _flash_attention(q.reshape(b, s, hd), kv_sh, kv_sh, heads, HEAD_DIM,
                                 v_block0=heads, fq=fq, fk=fk)
            w_out = fox_w_out[j].astype(BF16)
        h = _out_proj_residual(o.reshape(t, hd), w_out, h, g_a, s)
        last = layer == depth - 1
        h = _moe_layer(h, layer, s, ffn_norm_g[layer], sc_f, sh_f, g_f, router_w, router_b,
                       moe_w1, moe_b1, moe_w2, moe_b2, final_norm_g if last else None)
        if layer == n_a - 1:
            w_f = jnp.pad(shared_w_kvf[:, 2 * hd:], ((0, 0), (0, V7X_LANES - heads)))
            kv_sh, f_logit = _norm_matmul(h, shared_norm_g,
                                          shared_w_kvf[:, :2 * hd].astype(BF16), s,
                                          out_dtype=BF16, side_w=w_f)
            kv_sh = kv_sh.reshape(b, s, 2 * hd)
            b_pad = jnp.pad(shared_b_f, (0, V7X_LANES - heads)).reshape(1, V7X_LANES)
            fq = _forget_cumsum(f_logit.reshape(b, s, V7X_LANES), b_pad, LOG2E)
            fk = fq[:, :, :heads].transpose(0, 2, 1).reshape(b, heads, 1, s)
    return h.reshape(b, s, d)
```

```python
import functools

import jax
import jax.numpy as jnp
from jax import lax
from jax.experimental import pallas as pl
from jax.experimental.pallas import tpu as pltpu

F32 = jnp.float32
BF16 = jnp.bfloat16
HIGHEST = lax.Precision.HIGHEST

RMS_EPS = 1e-6
HEAD_DIM = 128
ROPE_DIM = 64
ROPE_THETA = 10000.0
TOP_K = 4
SWIGLU_ALPHA = 1.702
SWIGLU_LIMIT = 7.0
N_MOD = 6

V7X_LANES = 128
V7X_SUBLANES = 8
V7X_MXU_DIM = 256
V7X_VMEM_BYTES = 64 * 1024 * 1024
MIB = 1024 * 1024

NEG_BIG = -1e30

TOKEN_TILE = 256
WIDE_TOKEN_TILE = 512
COLUMN_CHUNK = 2048
MOD_CHUNK = 1024
FLASH_BLOCK = 512
MOE_ROW_TILE = 256
MOE_UP_CHUNK = 1024
COMBINE_TOKEN_TILE = 128


def _params(semantics, vmem_mib):
    assert vmem_mib * MIB < V7X_VMEM_BYTES
    return pltpu.CompilerParams(dimension_semantics=semantics,
                                vmem_limit_bytes=vmem_mib * MIB)


def _tile(n, pref):
    if n <= pref:
        return n
    t = pref
    while n % t:
        t //= 2
    return t


def _rms(x, g):
    ms = jnp.mean(x * x, axis=-1, keepdims=True)
    return x * lax.rsqrt(ms + RMS_EPS) * g


def _store_row_contiguous(ref, x):
    rows, d = x.shape
    lines = d // V7X_LANES
    for s in range(lines):
        ref[pl.ds(s, rows, stride=lines), :] = x[:, s * V7X_LANES:(s + 1) * V7X_LANES]


def _load_row_contiguous(ref, base, rows, lines):
    return jnp.concatenate(
        [ref[pl.ds(base + s, rows, stride=lines), :] for s in range(lines)], axis=1)


def _pack_bf16_pairs(x):
    half = x.shape[1] // 2
    hi = lax.bitcast_convert_type(x[:, :half].astype(BF16).astype(F32), jnp.uint32)
    lo = lax.bitcast_convert_type(x[:, half:].astype(BF16).astype(F32), jnp.uint32)
    return hi | (lo >> 16)


def _unpack_bf16_pairs(words):
    hi = lax.bitcast_convert_type(words & jnp.uint32(0xFFFF0000), F32)
    lo = lax.bitcast_convert_type(words << 16, F32)
    return hi, lo


def _packed_lines(d):
    return d // (2 * V7X_LANES)


def _mod_kernel(c_ref, w_ref, b_ref, o_ref):
    o_ref[...] = jnp.dot(c_ref[...], w_ref[...], precision=HIGHEST,
                         preferred_element_type=F32) + b_ref[...]


def _modulation(c_pad, mod_w, mod_b):
    n_layers, d, n = mod_w.shape
    rows = c_pad.shape[0]
    tn = _tile(n, MOD_CHUNK)
    return pl.pallas_call(
        _mod_kernel,
        grid=(n_layers, n // tn),
        in_specs=[
            pl.BlockSpec((rows, d), lambda l, j: (0, 0)),
            pl.BlockSpec((None, d, tn), lambda l, j: (l, 0, j)),
            pl.BlockSpec((None, 1, tn), lambda l, j: (l, 0, j)),
        ],
        out_specs=pl.BlockSpec((None, rows, tn), lambda l, j: (l, 0, j)),
        out_shape=jax.ShapeDtypeStruct((n_layers, rows, n), F32),
        compiler_params=_params(("arbitrary", "arbitrary"), 40),
        name="modulation",
    )(c_pad, mod_w, mod_b.reshape(n_layers, 1, n))


def _norm_matmul_kernel(*refs, modulate, side, out_scale):
    refs = list(refs)
    h_ref, g_ref = refs[:2]
    sc_ref, sh_ref = refs[2:4] if modulate else (None, None)
    w_ref = refs[4 if modulate else 2]
    ws_ref = refs[-3] if side else None
    o_ref, os_ref = (refs[-2], refs[-1]) if side else (refs[-1], None)
    y = _rms(h_ref[...], g_ref[...])
    if modulate:
        y = y * (1.0 + sc_ref[...]) + sh_ref[...]
    w = w_ref[...]
    if w.dtype == BF16:
        out = jnp.dot(y.astype(BF16), w, preferred_element_type=F32)
    else:
        out = jnp.dot(y, w, precision=HIGHEST, preferred_element_type=F32)
    if out_scale != 1.0:
        out = out * out_scale
    o_ref[...] = out.astype(o_ref.dtype)
    if side:
        os_ref[...] = jnp.dot(y, ws_ref[...], precision=HIGHEST, preferred_element_type=F32)


def _norm_matmul(h, g, w, seq, *, scale=None, shift=None, out_dtype=F32, out_scale=1.0,
                 side_w=None):
    t, d = h.shape
    n = w.shape[1]
    side = side_w is not None
    tm = _tile(seq, TOKEN_TILE)
    tn = n if side else _tile(n, COLUMN_CHUNK)
    per_batch = seq // tm
    modulate = scale is not None
    in_specs = [pl.BlockSpec((tm, d), lambda j, i: (i, 0)),
                pl.BlockSpec((1, d), lambda j, i: (0, 0))]
    args = [h, g.reshape(1, d)]
    if modulate:
        mod_spec = pl.BlockSpec((None, 1, d), lambda j, i: (i // per_batch, 0, 0))
        in_specs += [mod_spec, mod_spec]
        args += [scale, shift]
    in_specs.append(pl.BlockSpec((d, tn), lambda j, i: (0, j)))
    args.append(w)
    out_specs = pl.BlockSpec((tm, tn), lambda j, i: (i, j))
    out_shape = jax.ShapeDtypeStruct((t, n), out_dtype)
    if side:
        ns = side_w.shape[1]
        in_specs.append(pl.BlockSpec((d, ns), lambda j, i: (0, 0)))
        args.append(side_w)
        out_specs = [out_specs, pl.BlockSpec((tm, ns), lambda j, i: (i, 0))]
        out_shape = [out_shape, jax.ShapeDtypeStruct((t, ns), F32)]
    return pl.pallas_call(
        functools.partial(_norm_matmul_kernel, modulate=modulate, side=side,
                          out_scale=out_scale),
        grid=(n // tn, t // tm),
        in_specs=in_specs,
        out_specs=out_specs,
        out_shape=out_shape,
        compiler_params=_params(("arbitrary", "arbitrary"), 56 if side else 48),
        name="norm_matmul",
    )(*args)


def _rope_table_kernel(pos_ref, freq_ref, c_ref, s1_ref, s2_ref):
    half = ROPE_DIM // 2
    ang = pos_ref[...].astype(F32) * freq_ref[...]
    lane = lax.broadcasted_iota(jnp.int32, ang.shape, 1)
    cos = jnp.cos(ang)
    sin = jnp.sin(ang)
    c_ref[...] = jnp.where(lane < 2 * half, cos, 0.0)
    s1_ref[...] = jnp.where(lane < half, -sin, 0.0)
    s2_ref[...] = jnp.where((lane >= half) & (lane < 2 * half), sin, 0.0)


def _rope_tables(positions):
    t = positions.size
    half = ROPE_DIM // 2
    inv_freq = ROPE_THETA ** (-jnp.arange(0, ROPE_DIM, 2, dtype=F32) / ROPE_DIM)
    freq = jnp.tile(inv_freq, V7X_LANES // half).reshape(1, V7X_LANES)
    tm = _tile(t, WIDE_TOKEN_TILE)
    spec = pl.BlockSpec((tm, V7X_LANES), lambda i: (i, 0))
    shape = jax.ShapeDtypeStruct((t, V7X_LANES), F32)
    return pl.pallas_call(
        _rope_table_kernel,
        grid=(t // tm,),
        in_specs=[pl.BlockSpec((tm, 1), lambda i: (i, 0)),
                  pl.BlockSpec((1, V7X_LANES), lambda i: (0, 0))],
        out_specs=[spec, spec, spec],
        out_shape=[shape, shape, shape],
        compiler_params=_params(("arbitrary",), 16),
        name="rope_tables",
    )(positions.reshape(t, 1), freq)


def _rope_chunk(z, c, s1, s2):
    half = ROPE_DIM // 2
    return (z * c + pltpu.roll(z, V7X_LANES - half, axis=1) * s1
            + pltpu.roll(z, half, axis=1) * s2)


def _mla_q_kernel(x_ref, g_ref, w_ref, c_ref, s1_ref, s2_ref, o_ref, *, heads, scale):
    y = _rms(x_ref[...], g_ref[...]).astype(BF16)
    q = jnp.dot(y, w_ref[...], preferred_element_type=F32) * scale
    c, s1, s2 = c_ref[...], s1_ref[...], s2_ref[...]
    for h in range(heads):
        base = h * 2 * HEAD_DIM
        o_ref[:, base:base + HEAD_DIM] = q[:, base:base + HEAD_DIM].astype(BF16)
        z = q[:, base + HEAD_DIM:base + 2 * HEAD_DIM]
        o_ref[:, base + HEAD_DIM:base + 2 * HEAD_DIM] = _rope_chunk(z, c, s1, s2).astype(BF16)


def _mla_kv_kernel(x_ref, r_ref, g_ref, w_ref, c_ref, s1_ref, s2_ref, k_ref, v_ref, *, heads):
    y = _rms(x_ref[...], g_ref[...]).astype(BF16)
    kv = jnp.dot(y, w_ref[...], preferred_element_type=F32)
    kr = _rope_chunk(r_ref[...], c_ref[...], s1_ref[...], s2_ref[...]).astype(BF16)
    for h in range(heads):
        base = h * 2 * HEAD_DIM
        k_ref[:, base:base + HEAD_DIM] = kv[:, h * HEAD_DIM:(h + 1) * HEAD_DIM].astype(BF16)
        k_ref[:, base + HEAD_DIM:base + 2 * HEAD_DIM] = kr
    v_ref[...] = kv[:, heads * HEAD_DIM:].astype(BF16)


def _mla_qkv(proj, q_norm_g, kv_norm_g, wq, wkv, tables, heads, q_lora, kv_lora, scale):
    t = proj.shape[0]
    tm = _tile(t, TOKEN_TILE)
    dq = heads * 2 * HEAD_DIM
    assert q_lora == kv_lora and q_lora % V7X_LANES == 0
    tab_spec = pl.BlockSpec((tm, V7X_LANES), lambda i: (i, 0))
    rope_block = (q_lora + kv_lora) // V7X_LANES
    q = pl.pallas_call(
        functools.partial(_mla_q_kernel, heads=heads, scale=scale),
        grid=(t // tm,),
        in_specs=[pl.BlockSpec((tm, q_lora), lambda i: (i, 0)),
                  pl.BlockSpec((1, q_lora), lambda i: (0, 0)),
                  pl.BlockSpec((q_lora, dq), lambda i: (0, 0)),
                  tab_spec, tab_spec, tab_spec],
        out_specs=pl.BlockSpec((tm, dq), lambda i: (i, 0)),
        out_shape=jax.ShapeDtypeStruct((t, dq), BF16),
        compiler_params=_params(("arbitrary",), 40),
        name="mla_q",
    )(proj, q_norm_g.reshape(1, q_lora), wq, *tables)
    k, v = pl.pallas_call(
        functools.partial(_mla_kv_kernel, heads=heads),
        grid=(t // tm,),
        in_specs=[pl.BlockSpec((tm, kv_lora), lambda i: (i, 1)),
                  pl.BlockSpec((tm, V7X_LANES), lambda i: (i, rope_block)),
                  pl.BlockSpec((1, kv_lora), lambda i: (0, 0)),
                  pl.BlockSpec((kv_lora, dq), lambda i: (0, 0)),
                  tab_spec, tab_spec, tab_spec],
        out_specs=[pl.BlockSpec((tm, dq), lambda i: (i, 0)),
                   pl.BlockSpec((tm, heads * HEAD_DIM), lambda i: (i, 0))],
        out_shape=[jax.ShapeDtypeStruct((t, dq), BF16),
                   jax.ShapeDtypeStruct((t, heads * HEAD_DIM), BF16)],
        compiler_params=_params(("arbitrary",), 40),
        name="mla_kv",
    )(proj, proj, kv_norm_g.reshape(1, kv_lora), wkv, *tables)
    return q, k, v


FLASH_HEADS_PER_STEP = 4
LOG2E = 1.4426950408889634


def _flash_kernel(*refs, tq, dk, dv, hp, decay):
    if decay:
        q_ref, k_ref, v_ref, fq_ref, fk_ref, o_ref = refs
    else:
        q_ref, k_ref, v_ref, o_ref = refs
    qi = pl.program_id(2)
    qs = [q_ref[:, a * dk:(a + 1) * dk] for a in range(hp)]
    if decay:
        fq_all = fq_ref[...]
        lane = lax.broadcasted_iota(jnp.int32, fq_all.shape, 1)
        fqs = [jnp.sum(jnp.where(lane == pl.program_id(1) * hp + a, fq_all, 0.0), axis=-1,
                       keepdims=True) for a in range(hp)]

    def block(j, carry, masked):
        start = pl.multiple_of(j * tq, tq)
        out = []
        for a in range(hp):
            m, l, acc = carry[a]
            k = k_ref[pl.ds(start, tq), a * dk:(a + 1) * dk]
            v = v_ref[pl.ds(start, tq), a * dv:(a + 1) * dv]
            s = lax.dot_general(qs[a], k, (((1,), (1,)), ((), ())),
                                preferred_element_type=F32)
            if decay:
                s = s + (fqs[a] - fk_ref[a, :, pl.ds(start, tq)])
            if masked:
                row = lax.broadcasted_iota(jnp.int32, s.shape, 0)
                col = lax.broadcasted_iota(jnp.int32, s.shape, 1)
                s = jnp.where(col <= row, s, -jnp.inf)
            m_new = jnp.maximum(m, jnp.max(s, axis=-1, keepdims=True))
            p = jnp.exp2(s - m_new)
            alpha = jnp.exp2(m - m_new)
            l = alpha * l + jnp.sum(p, axis=-1, keepdims=True)
            acc = alpha * acc + jnp.dot(p.astype(BF16), v, preferred_element_type=F32)
            out.append((m_new, l, acc))
        return tuple(out)

    init = tuple((jnp.full((tq, 1), -jnp.inf, F32), jnp.zeros((tq, 1), F32),
                  jnp.zeros((tq, dv), F32)) for _ in range(hp))
    carry = lax.fori_loop(0, qi, lambda j, c: block(j, c, False), init)
    final = block(qi, carry, True)
    for a in range(hp):
        _, l, acc = final[a]
        o_ref[:, a * dv:(a + 1) * dv] = (acc / l).astype(o_ref.dtype)


def _flash_attention(q, k, v, heads, dv, v_block0=0, fq=None, fk=None):
    b, s, _ = q.shape
    dk = q.shape[-1] // heads
    tq = _tile(s, FLASH_BLOCK)
    hp = min(FLASH_HEADS_PER_STEP, heads)
    assert heads % hp == 0 and v_block0 % hp == 0
    vb0 = v_block0 // hp
    decay = fq is not None
    in_specs = [pl.BlockSpec((None, tq, hp * dk), lambda bi, h, i: (bi, i, h)),
                pl.BlockSpec((None, s, hp * dk), lambda bi, h, i: (bi, 0, h)),
                pl.BlockSpec((None, s, hp * dv), lambda bi, h, i: (bi, 0, vb0 + h))]
    args = [q, k, v]
    if decay:
        in_specs += [pl.BlockSpec((None, tq, V7X_LANES), lambda bi, h, i: (bi, i, 0)),
                     pl.BlockSpec((None, hp, 1, s), lambda bi, h, i: (bi, h, 0, 0))]
        args += [fq, fk]
    return pl.pallas_call(
        functools.partial(_flash_kernel, tq=tq, dk=dk, dv=dv, hp=hp, decay=decay),
        grid=(b, heads // hp, s // tq),
        in_specs=in_specs,
        out_specs=pl.BlockSpec((None, tq, hp * dv), lambda bi, h, i: (bi, i, h)),
        out_shape=jax.ShapeDtypeStruct((b, s, heads * dv), BF16),
        compiler_params=_params(("arbitrary", "arbitrary", "arbitrary"), 40),
        name="flash_attention",
    )(*args)


def _out_proj_kernel(o_ref, w_ref, h_ref, g_ref, out_ref):
    a = jnp.dot(o_ref[...], w_ref[...], preferred_element_type=F32)
    out_ref[...] = h_ref[...] + g_ref[...] * a


def _out_proj_residual(o, w, h, gate, seq):
    t, dk = o.shape
    d = w.shape[1]
    tm = _tile(seq, TOKEN_TILE)
    per_batch = seq // tm
    return pl.pallas_call(
        _out_proj_kernel,
        grid=(t // tm,),
        in_specs=[pl.BlockSpec((tm, dk), lambda i: (i, 0)),
                  pl.BlockSpec((dk, d), lambda i: (0, 0)),
                  pl.BlockSpec((tm, d), lambda i: (i, 0)),
                  pl.BlockSpec((None, 1, d), lambda i: (i // per_batch, 0, 0))],
        out_specs=pl.BlockSpec((tm, d), lambda i: (i, 0)),
        out_shape=jax.ShapeDtypeStruct((t, d), F32),
        compiler_params=_params(("arbitrary",), 40),
        name="out_proj_residual",
    )(o, w, h, gate)


def _forget_cumsum_kernel(f_ref, b_ref, o_ref, carry_ref, *, out_scale):
    @pl.when(pl.program_id(1) == 0)
    def _():
        carry_ref[...] = jnp.zeros_like(carry_ref)

    x = f_ref[...] + b_ref[...]
    log_f = jnp.minimum(x, 0.0) - jnp.log1p(jnp.exp(-jnp.abs(x)))
    ts = x.shape[0]
    row = lax.broadcasted_iota(jnp.int32, (ts, ts), 0)
    col = lax.broadcasted_iota(jnp.int32, (ts, ts), 1)
    tri = jnp.where(col <= row, 1.0, 0.0).astype(F32)
    c = jnp.dot(tri, log_f, precision=HIGHEST, preferred_element_type=F32) + carry_ref[...]
    o_ref[...] = c * out_scale
    carry_ref[...] = c[ts - 1:ts, :]


def _forget_cumsum(f_logit, b_pad, out_scale):
    b, s, lanes = f_logit.shape
    ts = _tile(s, WIDE_TOKEN_TILE)
    return pl.pallas_call(
        functools.partial(_forget_cumsum_kernel, out_scale=out_scale),
        grid=(b, s // ts),
        in_specs=[pl.BlockSpec((None, ts, lanes), lambda bi, i: (bi, i, 0)),
                  pl.BlockSpec((1, lanes), lambda bi, i: (0, 0))],
        out_specs=pl.BlockSpec((None, ts, lanes), lambda bi, i: (bi, i, 0)),
        out_shape=jax.ShapeDtypeStruct((b, s, lanes), F32),
        scratch_shapes=[pltpu.VMEM((1, lanes), F32)],
        compiler_params=_params(("arbitrary", "arbitrary"), 16),
        name="forget_cumsum",
    )(f_logit, b_pad)


def _moe_route_kernel(h_ref, g_ref, sc_ref, sh_ref, rwh_ref, rwl_ref, rb_ref, f_ref, idx_ref,
                      w_ref):
    y = _rms(h_ref[...], g_ref[...]) * (1.0 + sc_ref[...]) + sh_ref[...]
    _store_row_contiguous(f_ref, _pack_bf16_pairs(y))
    y_hi = y.astype(BF16)
    y_lo = (y - y_hi.astype(F32)).astype(BF16)
    rw_hi = rwh_ref[...]
    logits = (jnp.dot(y_hi, rw_hi, preferred_element_type=F32)
              + (jnp.dot(y_hi, rwl_ref[...], preferred_element_type=F32)
                 + jnp.dot(y_lo, rw_hi, preferred_element_type=F32))) + rb_ref[...]
    lane = lax.broadcasted_iota(jnp.int32, logits.shape, 1).astype(F32)
    vals, idxs = [], []
    for _ in range(TOP_K):
        m = jnp.max(logits, axis=-1, keepdims=True)
        ix = jnp.min(jnp.where(logits == m, lane, float(logits.shape[1])), axis=-1,
                     keepdims=True)
        vals.append(m)
        idxs.append(ix)
        logits = jnp.where(lane == ix, -jnp.inf, logits)
    exps = [jnp.exp(v - vals[0]) for v in vals]
    den = exps[0]
    for e in exps[1:]:
        den = den + e
    slot = lax.broadcasted_iota(jnp.int32, idx_ref.shape, 1)
    idx_out = jnp.zeros(idx_ref.shape, jnp.int32)
    w_out = jnp.zeros(w_ref.shape, F32)
    for k in range(TOP_K):
        idx_out = jnp.where(slot == k, idxs[k].astype(jnp.int32), idx_out)
        w_out = jnp.where(slot == k, exps[k] / den, w_out)
    idx_ref[...] = idx_out
    w_ref[...] = w_out


def _moe_route(h, g, scale, shift, rw_pad, rb_pad, seq):
    t, d = h.shape
    tm = _tile(seq, TOKEN_TILE)
    per_batch = seq // tm
    lanes = rw_pad.shape[1]
    lines = _packed_lines(d)
    mod_spec = pl.BlockSpec((None, 1, d), lambda i: (i // per_batch, 0, 0))
    rw_hi = rw_pad.astype(BF16)
    rw_lo = (rw_pad - rw_hi.astype(F32)).astype(BF16)
    return pl.pallas_call(
        _moe_route_kernel,
        grid=(t // tm,),
        in_specs=[pl.BlockSpec((tm, d), lambda i: (i, 0)),
                  pl.BlockSpec((1, d), lambda i: (0, 0)),
                  mod_spec, mod_spec,
                  pl.BlockSpec((d, lanes), lambda i: (0, 0)),
                  pl.BlockSpec((d, lanes), lambda i: (0, 0)),
                  pl.BlockSpec((1, lanes), lambda i: (0, 0))],
        out_specs=[pl.BlockSpec((tm * lines, V7X_LANES), lambda i: (i, 0)),
                   pl.BlockSpec((tm, TOP_K), lambda i: (i, 0)),
                   pl.BlockSpec((tm, TOP_K), lambda i: (i, 0))],
        out_shape=[jax.ShapeDtypeStruct((t * lines, V7X_LANES), jnp.uint32),
                   jax.ShapeDtypeStruct((t, TOP_K), jnp.int32),
                   jax.ShapeDtypeStruct((t, TOP_K), F32)],
        compiler_params=_params(("arbitrary",), 32),
        name="moe_route",
    )(h, g.reshape(1, d), scale, shift, rw_hi, rw_lo, rb_pad)


def _routing_tables(top_idx, n_experts, tm):
    t, k = top_idx.shape
    n_tiles = (t * k) // tm + n_experts
    e_flat = top_idx.reshape(-1)
    onehot = (e_flat[:, None] == jnp.arange(n_experts, dtype=jnp.int32)[None, :]).astype(jnp.int32)
    counts = jnp.sum(onehot, axis=0)
    count_start = jnp.cumsum(counts) - counts
    order = jnp.argsort(e_flat, stable=True).astype(jnp.int32)
    sorted_place = jnp.argsort(order).astype(jnp.int32)
    tiles_per = (counts + tm - 1) // tm
    tile_end = jnp.cumsum(tiles_per)
    tile_start = jnp.concatenate([jnp.zeros((1,), jnp.int32), tile_end]).astype(jnp.int32)
    row_start = tile_start[:-1] * tm
    pos = (row_start - count_start)[e_flat] + sorted_place
    n_used = tile_end[-1]
    tile_ids = jnp.arange(n_tiles, dtype=jnp.int32)
    tile_expert = jnp.sum((tile_ids[:, None] >= tile_end[None, :]).astype(jnp.int32), axis=1)
    tile_expert = jnp.minimum(tile_expert, n_experts - 1)
    in_tile = jnp.arange(tm, dtype=jnp.int32)[None, :]
    r = (tile_ids * tm - row_start[tile_expert])[:, None] + in_tile
    valid = r < counts[tile_expert][:, None]
    src_flat = order[jnp.clip(count_start[tile_expert][:, None] + r, 0, t * k - 1)]
    src_tok = jnp.where(valid, src_flat // k, 0).reshape(n_tiles * tm)
    return (src_tok.astype(jnp.int32), pos.reshape(t, k), tile_start,
            n_used.reshape(1).astype(jnp.int32))


ROW_DMA_UNROLL = 8


def _row_copy(src_hbm, row, buf, slot, r, sem):
    lines = src_hbm.shape[1]
    dst = buf.at[slot, pl.ds(pl.multiple_of(r * lines, lines), lines), :]
    return pltpu.make_async_copy(src_hbm.at[row], dst, sem.at[slot])


def _start_rows(src_hbm, idx_ref, n, buf, slot, sem):
    def body(i, c):
        for prio in range(2):
            r = 2 * i + prio
            _row_copy(src_hbm, idx_ref[0, 0, r], buf, slot, r, sem).start(priority=prio)
        return c
    assert n % 2 == 0
    lax.fori_loop(0, n // 2, body, 0, unroll=ROW_DMA_UNROLL // 2)


def _wait_rows(src_hbm, n, buf, slot, sem):
    def body(r, c):
        _row_copy(src_hbm, 0, buf, slot, r, sem).wait()
        return c
    lax.fori_loop(0, n, body, 0, unroll=ROW_DMA_UNROLL)


def _dispatch_kernel(nt_ref, cur_ref, nxt_ref, f_hbm, o_ref, buf, sem, *, tm):
    i = pl.program_id(0)
    nt = nt_ref[0]

    @pl.when(i == 0)
    def _():
        _start_rows(f_hbm, cur_ref, tm, buf, 0, sem)

    @pl.when(i + 1 < nt)
    def _():
        _start_rows(f_hbm, nxt_ref, tm, buf, (i + 1) % 2, sem)

    @pl.when(i < nt)
    def _():
        slot = i % 2
        _wait_rows(f_hbm, tm, buf, slot, sem)
        hi, lo = _unpack_bf16_pairs(_load_row_contiguous(buf.at[slot], 0, tm, f_hbm.shape[1]))
        half = hi.shape[1]
        o_ref[:, :half] = hi.astype(o_ref.dtype)
        o_ref[:, half:] = lo.astype(o_ref.dtype)

    @pl.when(i >= nt)
    def _():
        o_ref[...] = jnp.zeros_like(o_ref)


def _moe_dispatch(f_rows, src_tok, n_used, tm):
    t, lines, lanes = f_rows.shape
    d = 2 * lines * lanes
    n_tiles = src_tok.shape[0] // tm
    src3 = src_tok.reshape(n_tiles, 1, tm)
    smem = functools.partial(pl.BlockSpec, memory_space=pltpu.SMEM)
    grid_spec = pltpu.PrefetchScalarGridSpec(
        num_scalar_prefetch=1,
        grid=(n_tiles,),
        in_specs=[smem((1, 1, tm), lambda i, nt: (i, 0, 0)),
                  smem((1, 1, tm), lambda i, nt: (jnp.minimum(i + 1, n_tiles - 1), 0, 0)),
                  pl.BlockSpec(memory_space=pl.ANY)],
        out_specs=pl.BlockSpec((tm, d), lambda i, nt: (i, 0)),
        scratch_shapes=[pltpu.VMEM((2, tm * lines, lanes), jnp.uint32),
                        pltpu.SemaphoreType.DMA((2,))],
    )
    return pl.pallas_call(
        functools.partial(_dispatch_kernel, tm=tm),
        grid_spec=grid_spec,
        out_shape=jax.ShapeDtypeStruct((n_tiles * tm, d), BF16),
        compiler_params=_params(("arbitrary",), 32),
        name="moe_dispatch",
    )(n_used, src3, src3, f_rows)


TILE_DMA_PRIORITY = 0


def _run_expert_tiles(t0, t1, in_copies, out_copy, prologue, step):
    @pl.when(t1 > t0)
    def _():
        for cp in in_copies(t0, 0):
            cp.start(priority=TILE_DMA_PRIORITY)
        prologue()

        def body(t, c):
            slot = (t - t0) % 2
            for cp in in_copies(t, slot):
                cp.wait()

            @pl.when(t + 1 < t1)
            def _():
                for cp in in_copies(t + 1, 1 - slot):
                    cp.start(priority=TILE_DMA_PRIORITY)

            @pl.when(t - t0 >= 2)
            def _():
                out_copy(t - 2, slot).wait()

            step(slot)
            out_copy(t, slot).start(priority=TILE_DMA_PRIORITY)
            return c
        lax.fori_loop(t0, t1, body, 0)

        @pl.when(t1 - t0 >= 2)
        def _():
            out_copy(t1 - 2, (t1 - 2 - t0) % 2).wait()
        out_copy(t1 - 1, (t1 - 1 - t0) % 2).wait()


def _zero_tail_tiles(first, n_tiles, obuf, out_copy):
    obuf[0] = jnp.zeros(obuf.shape[1:], obuf.dtype)

    def body(t, c):
        out_copy(t, 0).start()
        out_copy(t, 0).wait()
        return c
    lax.fori_loop(first, n_tiles, body, 0)


WEIGHT_DMA_PRIORITY = 1


def _stream_weights(step, n_steps, copy_for_step, wbuf):
    slot = step % 2

    @pl.when(step == 0)
    def _():
        copy_for_step(step, 0).start(priority=WEIGHT_DMA_PRIORITY)

    @pl.when(step + 1 < n_steps)
    def _():
        copy_for_step(step + 1, 1 - slot).start(priority=WEIGHT_DMA_PRIORITY)

    copy_for_step(step, slot).wait()
    return wbuf.at[slot]


def _moe_up_kernel(ts_ref, x_hbm, w_hbm, bg_ref, bl_ref, o_hbm, wbuf, wbf_ref, xbuf, obuf,
                   wsem, xsem, osem, *, tm, n_tiles, layer):
    c = pl.program_id(0)
    e = pl.program_id(1)
    n_experts = pl.num_programs(1)
    half = wbf_ref.shape[1] // 2
    group = V7X_MXU_DIM

    def w_copy(step, slot):
        cols = pl.ds(pl.multiple_of((step // n_experts) * 2 * half, 2 * half), 2 * half)
        return pltpu.make_async_copy(w_hbm.at[layer, step % n_experts, :, cols],
                                     wbuf.at[slot], wsem.at[slot])

    w_ref = _stream_weights(c * n_experts + e, pl.num_programs(0) * n_experts, w_copy, wbuf)

    def in_copies(t, slot):
        return [pltpu.make_async_copy(x_hbm.at[t], xbuf.at[slot], xsem.at[slot])]

    def out_copy(t, slot):
        return pltpu.make_async_copy(obuf.at[slot], o_hbm.at[c, t], osem.at[slot])

    def convert():
        src = lax.broadcasted_iota(jnp.int32, (group, group), 0)
        dst = lax.broadcasted_iota(jnp.int32, (group, group), 1)
        want = jnp.where(dst < group // 2, 2 * dst, 2 * (dst - group // 2) + 1)
        perm = jnp.where(src == want, 1.0, 0.0).astype(BF16)
        for gi in range(wbf_ref.shape[1] // group):
            blk = w_ref[:, gi * group:(gi + 1) * group].astype(BF16)
            split = jnp.dot(blk, perm, preferred_element_type=F32).astype(BF16)
            lo = gi * (group // 2)
            wbf_ref[:, lo:lo + group // 2] = split[:, :group // 2]
            wbf_ref[:, half + lo:half + lo + group // 2] = split[:, group // 2:]

    def step(slot):
        u = jnp.dot(xbuf[slot], wbf_ref[...], preferred_element_type=F32)
        glu = jnp.minimum(u[:, :half] + bg_ref[...], SWIGLU_LIMIT)
        lin = jnp.clip(u[:, half:] + bl_ref[...], -SWIGLU_LIMIT, SWIGLU_LIMIT)
        act = glu * jax.nn.sigmoid(SWIGLU_ALPHA * glu) * (lin + 1.0)
        obuf[slot] = act.astype(obuf.dtype)

    _run_expert_tiles(ts_ref[e], ts_ref[e + 1], in_copies, out_copy, convert, step)

    @pl.when(e == pl.num_programs(1) - 1)
    def _():
        _zero_tail_tiles(ts_ref[e + 1], n_tiles, obuf, out_copy)


def _moe_down_kernel(ts_ref, a_hbm, w_hbm, b_ref, o_hbm, wbuf, wbf_ref, abuf, ybuf, wsem, asem,
                     ysem, *, tm, n_tiles, lines, layer):
    e = pl.program_id(0)
    tf = a_hbm.shape[3]

    def w_copy(step, slot):
        return pltpu.make_async_copy(w_hbm.at[layer, step], wbuf.at[slot], wsem.at[slot])

    w_ref = _stream_weights(e, pl.num_programs(0), w_copy, wbuf)

    def in_copies(t, slot):
        return [pltpu.make_async_copy(a_hbm.at[cc, t],
                                      abuf.at[slot, :, pl.ds(cc * tf, tf)], asem.at[slot])
                for cc in range(a_hbm.shape[0])]

    def out_copy(t, slot):
        return pltpu.make_async_copy(ybuf.at[slot], o_hbm.at[t], ysem.at[slot])

    def convert():
        wbf_ref[...] = w_ref[...].astype(BF16)

    def step(slot):
        y = jnp.dot(abuf[slot], wbf_ref[...], preferred_element_type=F32) + b_ref[...]
        _store_row_contiguous(ybuf.at[slot], _pack_bf16_pairs(y))

    _run_expert_tiles(ts_ref[e], ts_ref[e + 1], in_copies, out_copy, convert, step)

    @pl.when(e == pl.num_programs(0) - 1)
    def _():
        _zero_tail_tiles(ts_ref[e + 1], n_tiles, ybuf, out_copy)


def _moe_experts(x_sorted, tile_start, w1, b1g, b1l, w2, b2, layer, tm):
    p, d = x_sorted.shape
    n_experts, f = w2.shape[1], w2.shape[2]
    n_tiles = p // tm
    tf = _tile(f, MOE_UP_CHUNK)
    n_chunks = f // tf
    lines = _packed_lines(d)
    hbm = pl.BlockSpec(memory_space=pl.ANY)

    up_spec = pltpu.PrefetchScalarGridSpec(
        num_scalar_prefetch=1,
        grid=(n_chunks, n_experts),
        in_specs=[hbm, hbm,
                  pl.BlockSpec((None, None, 1, tf), lambda c, e, ts: (layer, e, 0, c)),
                  pl.BlockSpec((None, None, 1, tf), lambda c, e, ts: (layer, e, 0, c))],
        out_specs=hbm,
        scratch_shapes=[pltpu.VMEM((2, d, 2 * tf), F32), pltpu.VMEM((d, 2 * tf), BF16),
                        pltpu.VMEM((2, tm, d), BF16), pltpu.VMEM((2, tm, tf), BF16),
                        pltpu.SemaphoreType.DMA((2,)), pltpu.SemaphoreType.DMA((2,)),
                        pltpu.SemaphoreType.DMA((2,))],
    )
    act = pl.pallas_call(
        functools.partial(_moe_up_kernel, tm=tm, n_tiles=n_tiles, layer=layer),
        grid_spec=up_spec,
        out_shape=jax.ShapeDtypeStruct((n_chunks, n_tiles, tm, tf), BF16),
        compiler_params=_params(("arbitrary", "arbitrary"), 56),
        name="moe_up",
    )(tile_start, x_sorted.reshape(n_tiles, tm, d), w1, b1g, b1l)

    down_spec = pltpu.PrefetchScalarGridSpec(
        num_scalar_prefetch=1,
        grid=(n_experts,),
        in_specs=[hbm, hbm,
                  pl.BlockSpec((None, None, 1, d), lambda e, ts: (layer, e, 0, 0))],
        out_specs=hbm,
        scratch_shapes=[pltpu.VMEM((2, f, d), F32), pltpu.VMEM((f, d), BF16),
                        pltpu.VMEM((2, tm, f), BF16),
                        pltpu.VMEM((2, tm * lines, V7X_LANES), jnp.uint32),
                        pltpu.SemaphoreType.DMA((2,)), pltpu.SemaphoreType.DMA((2,)),
                        pltpu.SemaphoreType.DMA((2,))],
    )
    y = pl.pallas_call(
        functools.partial(_moe_down_kernel, tm=tm, n_tiles=n_tiles, lines=lines, layer=layer),
        grid_spec=down_spec,
        out_shape=jax.ShapeDtypeStruct((n_tiles, tm * lines, V7X_LANES), jnp.uint32),
        compiler_params=_params(("arbitrary",), 56),
        name="moe_down",
    )(tile_start, act, w2, b2)
    return y.reshape(p * lines, V7X_LANES)


def _combine_kernel(*refs, tc, final_norm):
    if final_norm:
        cur_ref, nxt_ref, y_hbm, h_ref, w_ref, g_ref, fg_ref, o_ref, buf, mix_ref, sem = refs
    else:
        cur_ref, nxt_ref, y_hbm, h_ref, w_ref, g_ref, o_ref, buf, mix_ref, sem = refs
    i = pl.program_id(0)
    n = pl.num_programs(0)
    rows = TOP_K * tc
    lines = y_hbm.shape[1]

    @pl.when(i == 0)
    def _():
        _start_rows(y_hbm, cur_ref, rows, buf, 0, sem)

    @pl.when(i + 1 < n)
    def _():
        _start_rows(y_hbm, nxt_ref, rows, buf, (i + 1) % 2, sem)

    slot = i % 2
    _wait_rows(y_hbm, rows, buf, slot, sem)

    w = w_ref[...]
    blk = tc * lines
    mix_hi = mix_lo = None
    for k in range(TOP_K):
        hi, lo = _unpack_bf16_pairs(buf[slot, pl.ds(k * blk, blk), :])
        wk = w[:, k:k + 1]
        mix_hi = wk * hi if mix_hi is None else mix_hi + wk * hi
        mix_lo = wk * lo if mix_lo is None else mix_lo + wk * lo
    mix_ref[0] = mix_hi
    mix_ref[1] = mix_lo
    mixed = jnp.concatenate([_load_row_contiguous(mix_ref.at[0], 0, tc, lines),
                             _load_row_contiguous(mix_ref.at[1], 0, tc, lines)], axis=1)
    out = h_ref[...] + g_ref[...] * mixed
    if final_norm:
        out = _rms(out, fg_ref[...])
    o_ref[...] = out


def _moe_combine(y_rows, pos, top_w, h, gate, seq, final_g=None):
    t, d = h.shape
    lines = _packed_lines(d)
    tc = _tile(seq, COMBINE_TOKEN_TILE)
    n = t // tc
    per_batch = seq // tc
    pos3 = pos.reshape(n, tc, TOP_K).transpose(0, 2, 1).reshape(n, 1, TOP_K * tc)
    w_lines = jnp.repeat(top_w, lines, axis=0)
    smem = functools.partial(pl.BlockSpec, memory_space=pltpu.SMEM)
    final_norm = final_g is not None
    in_specs = [smem((1, 1, TOP_K * tc), lambda i: (i, 0, 0)),
                smem((1, 1, TOP_K * tc), lambda i: (jnp.minimum(i + 1, n - 1), 0, 0)),
                pl.BlockSpec(memory_space=pl.ANY),
                pl.BlockSpec((tc, d), lambda i: (i, 0)),
                pl.BlockSpec((tc * lines, TOP_K), lambda i: (i, 0)),
                pl.BlockSpec((None, 1, d), lambda i: (i // per_batch, 0, 0))]
    args = [pos3, pos3, y_rows, h, w_lines, gate]
    if final_norm:
        in_specs.append(pl.BlockSpec((1, d), lambda i: (0, 0)))
        args.append(final_g.reshape(1, d))
    return pl.pallas_call(
        functools.partial(_combine_kernel, tc=tc, final_norm=final_norm),
        grid=(n,),
        in_specs=in_specs,
        out_specs=pl.BlockSpec((tc, d), lambda i: (i, 0)),
        out_shape=jax.ShapeDtypeStruct((t, d), F32),
        scratch_shapes=[pltpu.VMEM((2, TOP_K * tc * lines, V7X_LANES), jnp.uint32),
                        pltpu.VMEM((2, tc * lines, V7X_LANES), F32),
                        pltpu.SemaphoreType.DMA((2,))],
        compiler_params=_params(("arbitrary",), 32),
        name="moe_combine",
    )(*args)


def _moe_layer(h, layer, seq, norm_g, scale, shift, gate, router_w, router_b,
               moe_w1, moe_b1, moe_w2, moe_b2, final_g=None):
    n_experts = router_w.shape[-1]
    d = h.shape[1]
    lanes = V7X_LANES * pl.cdiv(n_experts, V7X_LANES)
    rw_pad = jnp.pad(router_w[layer], ((0, 0), (0, lanes - n_experts)))
    rb_pad = jnp.pad(router_b[layer], (0, lanes - n_experts),
                     constant_values=NEG_BIG).reshape(1, lanes)
    f_in, top_idx, top_w = _moe_route(h, norm_g, scale, shift, rw_pad, rb_pad, seq)
    tm = MOE_ROW_TILE
    src_tok, pos, tile_start, n_used = _routing_tables(top_idx, n_experts, tm)
    lines = _packed_lines(d)
    x_sorted = _moe_dispatch(f_in.reshape(-1, lines, V7X_LANES), src_tok, n_used, tm)
    f2 = moe_b1.shape[-1]
    b1 = moe_b1.reshape(moe_b1.shape[0], n_experts, f2 // 2, 2)
    b1g = b1[..., 0].reshape(moe_b1.shape[0], n_experts, 1, f2 // 2)
    b1l = b1[..., 1].reshape(moe_b1.shape[0], n_experts, 1, f2 // 2)
    b2 = moe_b2.reshape(moe_b2.shape[0], n_experts, 1, d)
    y_sorted = _moe_experts(x_sorted, tile_start, moe_w1, b1g, b1l, moe_w2, b2, layer, tm)
    return _moe_combine(y_sorted.reshape(-1, lines, V7X_LANES), pos, top_w, h, gate, seq,
                        final_g)


def _mla_weight_layouts(w_in, w_q_up, w_kv_up, heads, q_lora, kv_lora):
    half = ROPE_DIM // 2
    pad = HEAD_DIM - ROPE_DIM
    d = w_in.shape[0]
    rope = w_in[:, q_lora + kv_lora:]
    w_in_p = jnp.concatenate(
        [w_in[:, :q_lora + kv_lora], rope[:, 0::2], rope[:, 1::2], jnp.zeros((d, pad), F32)],
        axis=1).astype(BF16)
    wq = w_q_up.reshape(q_lora, heads, HEAD_DIM + ROPE_DIM)
    wq_rope = wq[:, :, HEAD_DIM:]
    wq_p = jnp.concatenate(
        [wq[:, :, :HEAD_DIM], wq_rope[:, :, 0::2], wq_rope[:, :, 1::2],
         jnp.zeros((q_lora, heads, pad), F32)], axis=2)
    wq_p = wq_p.reshape(q_lora, heads * 2 * HEAD_DIM).astype(BF16)
    wkv = w_kv_up.reshape(kv_lora, heads, 2 * HEAD_DIM)
    wkv_p = jnp.concatenate([wkv[:, :, :HEAD_DIM].reshape(kv_lora, heads * HEAD_DIM),
                             wkv[:, :, HEAD_DIM:].reshape(kv_lora, heads * HEAD_DIM)],
                            axis=1).astype(BF16)
    assert half * 2 == ROPE_DIM and w_in_p.shape[1] == q_lora + kv_lora + HEAD_DIM
    return w_in_p, wq_p, wkv_p


def kernel(x, c, positions, mod_w, mod_b, attn_norm_g, ffn_norm_g, mla_w_in, mla_q_norm_g,
           mla_kv_norm_g, mla_w_q_up, mla_w_kv_up, mla_w_out, shared_norm_g, shared_w_kvf,
           shared_b_f, fox_w_q, fox_w_out, router_w, router_b, moe_w1, moe_b1, moe_w2, moe_b2,
           final_norm_g):
    b, s, d = x.shape
    depth = mod_w.shape[0]
    n_a = mla_w_in.shape[0]
    heads = d // HEAD_DIM
    q_lora = mla_q_norm_g.shape[-1]
    kv_lora = mla_kv_norm_g.shape[-1]
    hd = heads * HEAD_DIM
    t = b * s

    c_pad = jnp.pad(c, ((0, (-b) % V7X_SUBLANES), (0, 0)))
    mod = _modulation(c_pad, mod_w, mod_b)[:, :b, :]
    tables = _rope_tables(positions)

    h = x.reshape(t, d)
    kv_sh = fq = fk = None
    for layer in range(depth):
        sh_a, sc_a, g_a, sh_f, sc_f, g_f = (
            mod[layer, :, i * d:(i + 1) * d].reshape(b, 1, d) for i in range(N_MOD))
        if layer < n_a:
            w_in_p, wq_p, wkv_p = _mla_weight_layouts(
                mla_w_in[layer], mla_w_q_up[layer], mla_w_kv_up[layer], heads, q_lora, kv_lora)
            proj = _norm_matmul(h, attn_norm_g[layer], w_in_p, s, scale=sc_a, shift=sh_a)
            q, k, v = _mla_qkv(proj, mla_q_norm_g[layer], mla_kv_norm_g[layer], wq_p, wkv_p,
                               tables, heads, q_lora, kv_lora,
                               LOG2E * (HEAD_DIM + ROPE_DIM) ** -0.5)
            o = _flash_attention(q.reshape(b, s, -1), k.reshape(b, s, -1),
                                 v.reshape(b, s, -1), heads, HEAD_DIM)
            w_out = mla_w_out[layer].astype(BF16)
        else:
            j = layer - n_a
            q = _norm_matmul(h, attn_norm_g[layer], fox_w_q[j].astype(BF16), s, scale=sc_a,
                             shift=sh_a, out_dtype=BF16,
                             out_scale=LOG2E * HEAD_DIM ** -0.5)
            o = _flash_attention(q.reshape(b, s, hd), kv_sh, kv_sh, heads, HEAD_DIM,
                                 v_block0=heads, fq=fq, fk=fk)
            w_out = fox_w_out[j].astype(BF16)
        h = _out_proj_residual(o.reshape(t, hd), w_out, h, g_a, s)
        last = layer == depth - 1
        h = _moe_layer(h, layer, s, ffn_norm_g[layer], sc_f, sh_f, g_f, router_w, router_b,
                       moe_w1, moe_b1, moe_w2, moe_b2, final_norm_g if last else None)
        if layer == n_a - 1:
            w_f = jnp.pad(shared_w_kvf[:, 2 * hd:], ((0, 0), (0, V7X_LANES - heads)))
            kv_sh, f_logit = _norm_matmul(h, shared_norm_g,
                                          shared_w_kvf[:, :2 * hd].astype(BF16), s,
                                          out_dtype=BF16, side_w=w_f)
            kv_sh = kv_sh.reshape(b, s, 2 * hd)
            b_pad = jnp.pad(shared_b_f, (0, V7X_LANES - heads)).reshape(1, V7X_LANES)
            fq = _forget_cumsum(f_logit.reshape(b, s, V7X_LANES), b_pad, LOG2E)
            fk = fq[:, :, :heads].transpose(0, 2, 1).reshape(b, heads, 1, s)
    return h.reshape(b, s, d)
```
